```python
import math
import jax, jax.numpy as jnp
from jax import lax
import numpy as np

D_MODEL = 1024
BATCH = 16
SEQ = 4096
DEPTH = 2
DEC_BATCH = 4
DEC_SEQ = 4096
PAST_LEN = 128

N_HEADS = 8
N_KV_HEADS = 2
Q_GROUP = N_HEADS // N_KV_HEADS
HEAD_DIM = 64
WINDOW = 128
ATTN_BLOCK = 128
ATTN_WIDTH = N_HEADS * HEAD_DIM
KV_WIDTH = N_KV_HEADS * HEAD_DIM
N_FOURIER_GROUPS = 8
FOURIER_GROUP_DIM = 64
FOURIER_WIDTH = N_FOURIER_GROUPS * FOURIER_GROUP_DIM
N_BRANCHES = 2
IN_WIDTH = FOURIER_WIDTH + ATTN_WIDTH + 2 * KV_WIDTH + N_BRANCHES * D_MODEL
PEER_HEADS = 8
PEER_NKEYS = 128
PEER_N_EXPERTS = PEER_NKEYS * PEER_NKEYS
PEER_TOPK = 16
PEER_KEY_DIM = 128
PEER_CHUNK = 128
PLE_DIM = 256
DEEPNORM_ALPHA = (2 * DEPTH) ** 0.25
DEEPNORM_BETA = (8 * DEPTH) ** -0.25
LN_EPS = 1e-5
NEG_INF = -1e30

kernel_name = 'hybrid_fnet_swa_peer_encoder'


def layer_norm(x, g, b):
    xf = x.astype(jnp.float32)
    mu = jnp.mean(xf, axis=-1, keepdims=True)
    var = jnp.mean(jnp.square(xf - mu), axis=-1, keepdims=True)
    y = (xf - mu) * lax.rsqrt(var + LN_EPS) * g.astype(jnp.float32) + b.astype(jnp.float32)
    return y.astype(x.dtype)


def alibi_slopes():
    h = jnp.arange(1, N_HEADS + 1, dtype=jnp.float32)
    return jnp.exp2(-8.0 * h / N_HEADS)


def fourier_mix(f):
    B, S, _ = f.shape
    fg = f.reshape(B, S, N_FOURIER_GROUPS, FOURIER_GROUP_DIM).astype(jnp.float32)
    out = jnp.fft.fftn(fg, axes=(1, 3), norm='ortho').real
    return out.reshape(B, S, FOURIER_WIDTH).astype(f.dtype)


def window_attention(q, k, v, sink):
    B, S, _ = q.shape
    nb = S // ATTN_BLOCK
    f32 = jnp.float32
    qb = q.reshape(B, nb, ATTN_BLOCK, N_KV_HEADS, Q_GROUP, HEAD_DIM).astype(f32)

    def neighbours(t):
        tp = jnp.pad(t.reshape(B, S, N_KV_HEADS, HEAD_DIM).astype(f32),
                     ((0, 0), (ATTN_BLOCK, ATTN_BLOCK), (0, 0), (0, 0)))
        tp = tp.reshape(B, nb + 2, ATTN_BLOCK, N_KV_HEADS, HEAD_DIM)
        return jnp.concatenate([tp[:, :-2], tp[:, 1:-1], tp[:, 2:]], axis=2)

    kb = neighbours(k)
    vb = neighbours(v)
    s = jnp.einsum('bnqkgd,bnskd->bnkgqs', qb, kb) * (HEAD_DIM ** -0.5)
    qi = jnp.arange(ATTN_BLOCK)
    kj = jnp.arange(3 * ATTN_BLOCK)
    rel = qi[:, None] + ATTN_BLOCK - kj[None, :]
    kpos = (jnp.arange(nb)[:, None] - 1) * ATTN_BLOCK + kj[None, :]
    valid = (jnp.abs(rel) <= WINDOW)[None] & ((kpos >= 0) & (kpos < S))[:, None, :]
    slopes = alibi_slopes().reshape(N_KV_HEADS, Q_GROUP)
    bias = -slopes[:, :, None, None] * jnp.abs(rel).astype(f32)[None, None]
    bias = jnp.where(valid[:, None, None], bias[None], NEG_INF)
    s = s + bias[None]
    sk = sink.astype(f32).reshape(N_KV_HEADS, Q_GROUP)[None, None, :, :, None]
    m = jnp.maximum(jnp.max(s, axis=-1), sk)
    pr = jnp.exp(s - m[..., None])
    denom = jnp.sum(pr, axis=-1) + jnp.exp(sk - m)
    o = jnp.einsum('bnkgqs,bnskd->bnkgqd', pr, vb) / denom[..., None]
    return o.transpose(0, 1, 4, 2, 3, 5).reshape(B, S, ATTN_WIDTH).astype(q.dtype)


def peer(x, w_q, keys, u, v):
    B, S, D = x.shape
    f32 = jnp.float32
    xt = x.reshape(-1, PEER_CHUNK, D)

    def chunk(xc):
        C = xc.shape[0]
        qh = (xc @ w_q).reshape(C, PEER_HEADS, 2, PEER_KEY_DIM).astype(f32)
        sc = jnp.einsum('chpd,hpnd->chpn', qh, keys.astype(f32))
        s1, i1 = lax.top_k(sc[:, :, 0], PEER_TOPK)
        s2, i2 = lax.top_k(sc[:, :, 1], PEER_TOPK)
        cand = (s1[..., :, None] + s2[..., None, :]).reshape(C, PEER_HEADS, PEER_TOPK * PEER_TOPK)
        cidx = (i1[..., :, None] * PEER_NKEYS + i2[..., None, :]).reshape(C, PEER_HEADS, PEER_TOPK * PEER_TOPK)
        fs, fi = lax.top_k(cand, PEER_TOPK)
        eidx = jnp.take_along_axis(cidx, fi, axis=-1)
        gate = jax.nn.softmax(fs, axis=-1)
        ue = u[eidx].astype(f32)
        act = jax.nn.gelu(jnp.einsum('cd,chkd->chk', xc.astype(f32), ue), approximate=False)
        ve = v[eidx].astype(f32)
        return jnp.einsum('chk,chkd->cd', gate * act, ve).astype(x.dtype)

    return lax.map(chunk, xt).reshape(B, S, D)


def encoder_layer(x, p, w_in, sink, w_fo, w_ao, w_out, ln1_g, ln1_b,
                  pq, pkeys, pu, pv, ple_wg, ple_wp, ln2_g, ln2_b):
    h = x @ w_in
    o0 = FOURIER_WIDTH
    o1 = o0 + ATTN_WIDTH
    o2 = o1 + KV_WIDTH
    o3 = o2 + KV_WIDTH
    f_in, q, k, v, g = h[..., :o0], h[..., o0:o1], h[..., o1:o2], h[..., o2:o3], h[..., o3:]
    f_branch = fourier_mix(f_in) @ w_fo
    a_branch = window_attention(q, k, v, sink) @ w_ao
    gates = jax.nn.sigmoid(g.astype(jnp.float32)).astype(x.dtype)
    merged = gates[..., :D_MODEL] * f_branch + gates[..., D_MODEL:] * a_branch
    x = layer_norm(DEEPNORM_ALPHA * x + merged @ w_out, ln1_g, ln1_b)
    cm = peer(x, pq, pkeys, pu, pv)
    ple = jax.nn.sigmoid((x @ ple_wg).astype(jnp.float32)).astype(x.dtype) * (p @ ple_wp)
    return layer_norm(DEEPNORM_ALPHA * x + cm + ple, ln2_g, ln2_b)


def trunk(x, p_all, emb_ln_g, emb_ln_b, w_in, attn_sink, w_fourier_out, w_attn_out, w_out,
          ln1_g, ln1_b, peer_w_q, peer_keys, peer_u, peer_v, ple_w_gate, ple_w_proj, ln2_g, ln2_b):
    x = layer_norm(x, emb_ln_g, emb_ln_b)
    for i in range(DEPTH):
        x = encoder_layer(x, p_all[i], w_in[i], attn_sink[i], w_fourier_out[i], w_attn_out[i],
                          w_out[i], ln1_g[i], ln1_b[i], peer_w_q[i], peer_keys[i], peer_u[i],
                          peer_v[i], ple_w_gate[i], ple_w_proj[i], ln2_g[i], ln2_b[i])
    return x


def setup_inputs(seed: int = 0) -> dict:
    key = jax.random.key(seed)
    ks = jax.random.split(key, 24)
    f32 = jnp.float32

    def nrm(k, shape, scale):
        return jax.random.normal(k, shape, f32) * scale

    return {
        'x_prompt': nrm(ks[0], (BATCH, SEQ, D_MODEL), 1.0),
        'x_sample': nrm(ks[1], (DEC_BATCH, DEC_SEQ, D_MODEL), 1.0),
        'p_prompt': nrm(ks[2], (DEPTH, BATCH, SEQ, PLE_DIM), 1.0),
        'p_sample': nrm(ks[3], (DEPTH, DEC_BATCH, DEC_SEQ, PLE_DIM), 1.0),
        'emb_ln_g': 1.0 + nrm(ks[4], (D_MODEL,), 0.02),
        'emb_ln_b': nrm(ks[5], (D_MODEL,), 0.02),
        'w_in': nrm(ks[6], (DEPTH, D_MODEL, IN_WIDTH), D_MODEL ** -0.5),
        'attn_sink': nrm(ks[7], (DEPTH, N_HEADS), 1.0),
        'w_fourier_out': nrm(ks[8], (DEPTH, FOURIER_WIDTH, D_MODEL), FOURIER_WIDTH ** -0.5),
        'w_attn_out': nrm(ks[9], (DEPTH, ATTN_WIDTH, D_MODEL), ATTN_WIDTH ** -0.5),
        'w_out': nrm(ks[10], (DEPTH, D_MODEL, D_MODEL), DEEPNORM_BETA * D_MODEL ** -0.5),
        'ln1_g': 1.0 + nrm(ks[11], (DEPTH, D_MODEL), 0.02),
        'ln1_b': nrm(ks[12], (DEPTH, D_MODEL), 0.02),
        'peer_w_q': nrm(ks[13], (DEPTH, D_MODEL, PEER_HEADS * 2 * PEER_KEY_DIM), D_MODEL ** -0.5),
        'peer_keys': nrm(ks[14], (DEPTH, PEER_HEADS, 2, PEER_NKEYS, PEER_KEY_DIM), PEER_KEY_DIM ** -0.5),
        'peer_u': nrm(ks[15], (DEPTH, PEER_N_EXPERTS, D_MODEL), D_MODEL ** -0.5),
        'peer_v': nrm(ks[16], (DEPTH, PEER_N_EXPERTS, D_MODEL), DEEPNORM_BETA),
        'ple_w_gate': nrm(ks[17], (DEPTH, D_MODEL, D_MODEL), D_MODEL ** -0.5),
        'ple_w_proj': nrm(ks[18], (DEPTH, PLE_DIM, D_MODEL), DEEPNORM_BETA * PLE_DIM ** -0.5),
        'ln2_g': 1.0 + nrm(ks[19], (DEPTH, D_MODEL), 0.02),
        'ln2_b': nrm(ks[20], (DEPTH, D_MODEL), 0.02),
    }


def reference(x_prompt, x_sample, p_prompt, p_sample, emb_ln_g, emb_ln_b, w_in, attn_sink,
              w_fourier_out, w_attn_out, w_out, ln1_g, ln1_b, peer_w_q, peer_keys, peer_u,
              peer_v, ple_w_gate, ple_w_proj, ln2_g, ln2_b):
    y_prompt = trunk(x_prompt, p_prompt, emb_ln_g, emb_ln_b, w_in, attn_sink, w_fourier_out,
                     w_attn_out, w_out, ln1_g, ln1_b, peer_w_q, peer_keys, peer_u, peer_v,
                     ple_w_gate, ple_w_proj, ln2_g, ln2_b)
    y_sample = trunk(x_sample, p_sample, emb_ln_g, emb_ln_b, w_in, attn_sink, w_fourier_out,
                     w_attn_out, w_out, ln1_g, ln1_b, peer_w_q, peer_keys, peer_u, peer_v,
                     ple_w_gate, ple_w_proj, ln2_g, ln2_b)
    return (y_prompt, y_sample)
```

```python
import functools
import math

import numpy as np
import jax
import jax.numpy as jnp
from jax import lax
from jax.experimental import pallas as pl
from jax.experimental.pallas import tpu as pltpu

F32 = jnp.float32
BF16 = jnp.bfloat16

D_MODEL = 1024
N_HEADS = 8
N_KV_HEADS = 2
Q_GROUP = N_HEADS // N_KV_HEADS
HEAD_DIM = 64
WINDOW = 128
ATTN_BLOCK = 128
ATTN_WIDTH = N_HEADS * HEAD_DIM
KV_WIDTH = N_KV_HEADS * HEAD_DIM
N_FOURIER_GROUPS = 8
FOURIER_GROUP_DIM = 64
FOURIER_WIDTH = N_FOURIER_GROUPS * FOURIER_GROUP_DIM
GATE_WIDTH = 2 * D_MODEL
IN_WIDTH = FOURIER_WIDTH + ATTN_WIDTH + 2 * KV_WIDTH + GATE_WIDTH
PEER_HEADS = 8
PEER_NKEYS = 128
PEER_N_EXPERTS = PEER_NKEYS * PEER_NKEYS
PEER_TOPK = 16
PEER_KEY_DIM = 128
PLE_DIM = 256
DEPTH = 2
DEEPNORM_ALPHA = (2 * DEPTH) ** 0.25
LN_EPS = 1e-5
NEG_INF = -1e30

LANES = 128
SUBLANES = 8
VMEM_LIMIT = 52 * 1024 * 1024

_O_F = 0
_O_Q = _O_F + FOURIER_WIDTH
_O_K = _O_Q + ATTN_WIDTH
_O_V = _O_K + KV_WIDTH
_O_G = _O_V + KV_WIDTH

_CAND_GROUPS = [(0, 0), (0, 8), (1, 0), (2, 0), (3, 0), (4, 0), (5, 0), (6, 0), (7, 0)]
_CAND_ROWS = 8 * (len(_CAND_GROUPS) + 1)
_BIG_INDEX = 1.0e6


def _params(semantics):
    return pltpu.CompilerParams(dimension_semantics=semantics, vmem_limit_bytes=VMEM_LIMIT)


def _layer_norm(x, g, b):
    mu = jnp.mean(x, axis=-1, keepdims=True)
    xc = x - mu
    var = jnp.mean(xc * xc, axis=-1, keepdims=True)
    return xc * lax.rsqrt(var + LN_EPS) * g + b


def _dot(a, b):
    return jnp.dot(a, b, preferred_element_type=F32)


def _dot_nt(a, b):
    return lax.dot_general(a, b, (((1,), (1,)), ((), ())), preferred_element_type=F32)


def _inproj_kernel(*refs, apply_ln):
    if apply_ln:
        x_ref, g_ref, b_ref, w_ref, mch_ref, xn_ref, ab_ref, q_ref, k_ref, v_ref, gt_ref = refs
        x = _layer_norm(x_ref[...], g_ref[...], b_ref[...])
        xn_ref[...] = x
    else:
        x_ref, w_ref, mch_ref, ab_ref, q_ref, k_ref, v_ref, gt_ref = refs
        x = x_ref[...]
    xb = x.astype(BF16)
    f = _dot(xb, w_ref[:, _O_F:_O_Q])
    ab_ref[...] = _dot(f.astype(BF16), mch_ref[...]).astype(BF16)
    q_ref[...] = _dot(xb, w_ref[:, _O_Q:_O_K]).astype(BF16)
    k_ref[...] = _dot(xb, w_ref[:, _O_K:_O_V]).astype(BF16)
    v_ref[...] = _dot(xb, w_ref[:, _O_V:_O_G]).astype(BF16)
    gt_ref[...] = jax.nn.sigmoid(_dot(xb, w_ref[:, _O_G:IN_WIDTH])).astype(BF16)


def _inproj(x, ln, w_in, mch, tile):
    t = x.shape[0]
    apply_ln = ln is not None
    row = lambda width: pl.BlockSpec((tile, width), lambda i: (i, 0))
    full = lambda a: pl.BlockSpec(a.shape, lambda i: (0,) * a.ndim)
    ins = [x] + ([ln[0], ln[1]] if apply_ln else []) + [w_in, mch]
    in_specs = [row(D_MODEL)] + ([full(ln[0]), full(ln[1])] if apply_ln else []) + [full(w_in), full(mch)]
    widths = [2 * FOURIER_WIDTH, ATTN_WIDTH, KV_WIDTH, KV_WIDTH, GATE_WIDTH]
    out_shape = [jax.ShapeDtypeStruct((t, w), BF16) for w in widths]
    out_specs = [row(w) for w in widths]
    if apply_ln:
        out_shape = [jax.ShapeDtypeStruct((t, D_MODEL), F32)] + out_shape
        out_specs = [row(D_MODEL)] + out_specs
    return pl.pallas_call(
        functools.partial(_inproj_kernel, apply_ln=apply_ln),
        grid=(t // tile,),
        in_specs=in_specs,
        out_specs=out_specs,
        out_shape=out_shape,
        compiler_params=_params(("parallel",)),
        name="inproj_ln" if apply_ln else "inproj",
    )(*ins)


def _seqdft_kernel(c_ref, s_ref, a_ref, b_ref, y_ref):
    y_ref[...] = (_dot(c_ref[...], a_ref[...]) + _dot(s_ref[...], b_ref[...])).astype(BF16)


def _seqdft(ab, cmat, smat, tile):
    bsz, s, _ = ab.shape
    return pl.pallas_call(
        _seqdft_kernel,
        grid=(bsz, s // tile),
        in_specs=[
            pl.BlockSpec((tile, s), lambda b, m: (m, 0)),
            pl.BlockSpec((tile, s), lambda b, m: (m, 0)),
            pl.BlockSpec((None, s, FOURIER_WIDTH), lambda b, m: (b, 0, 0)),
            pl.BlockSpec((None, s, FOURIER_WIDTH), lambda b, m: (b, 0, 1)),
        ],
        out_specs=pl.BlockSpec((None, tile, FOURIER_WIDTH), lambda b, m: (b, m, 0)),
        out_shape=jax.ShapeDtypeStruct((bsz, s, FOURIER_WIDTH), BF16),
        compiler_params=_params(("parallel", "parallel")),
        name="seqdft",
    )(cmat, smat, ab, ab)


def _attn_kernel(sink_ref, q_ref, kp_ref, kc_ref, kn_ref, vp_ref, vc_ref, vn_ref, bias_ref, o_ref, *, nblocks):
    n = pl.program_id(1)
    col = lax.broadcasted_iota(jnp.int32, (1, 3 * ATTN_BLOCK), 1)
    off_edge = ((col < ATTN_BLOCK) & (n == 0)) | ((col >= 2 * ATTN_BLOCK) & (n == nblocks - 1))
    edge = jnp.where(off_edge, NEG_INF, 0.0).astype(F32)
    kcat = jnp.concatenate([kp_ref[...], kc_ref[...], kn_ref[...]], axis=0)
    vcat = jnp.concatenate([vp_ref[...], vc_ref[...], vn_ref[...]], axis=0)
    lane = lax.broadcasted_iota(jnp.int32, (ATTN_BLOCK, KV_WIDTH), 1)
    low = lane < HEAD_DIM
    for g in range(Q_GROUP):
        qg = q_ref[:, g * KV_WIDTH:(g + 1) * KV_WIDTH]
        halves = []
        for kh in range(N_KV_HEADS):
            h = kh * Q_GROUP + g
            qm = jnp.where(low if kh == 0 else jnp.logical_not(low), qg, jnp.zeros_like(qg))
            s = _dot_nt(qm, kcat) + bias_ref[h] + edge
            sink = sink_ref[h]
            m = jnp.maximum(jnp.max(s, axis=-1, keepdims=True), sink)
            p = jnp.exp(s - m)
            denom = jnp.sum(p, axis=-1, keepdims=True) + jnp.exp(sink - m)
            halves.append(_dot(p.astype(BF16), vcat) / denom)
        o_ref[:, g * KV_WIDTH:(g + 1) * KV_WIDTH] = jnp.where(low, halves[0], halves[1]).astype(BF16)


def _attention(q, k, v, bias, sink):
    bsz, s, _ = q.shape
    nb = s // ATTN_BLOCK
    kv_spec = lambda fn: pl.BlockSpec((None, ATTN_BLOCK, KV_WIDTH), fn)
    prev = lambda b, n: (b, jnp.maximum(n - 1, 0), 0)
    cur = lambda b, n: (b, n, 0)
    nxt = lambda b, n: (b, jnp.minimum(n + 1, nb - 1), 0)
    return pl.pallas_call(
        functools.partial(_attn_kernel, nblocks=nb),
        grid=(bsz, nb),
        in_specs=[
            pl.BlockSpec(memory_space=pltpu.SMEM),
            pl.BlockSpec((None, ATTN_BLOCK, ATTN_WIDTH), cur),
            kv_spec(prev), kv_spec(cur), kv_spec(nxt),
            kv_spec(prev), kv_spec(cur), kv_spec(nxt),
            pl.BlockSpec(bias.shape, lambda b, n: (0, 0, 0)),
        ],
        out_specs=pl.BlockSpec((None, ATTN_BLOCK, ATTN_WIDTH), cur),
        out_shape=jax.ShapeDtypeStruct((bsz, s, ATTN_WIDTH), BF16),
        compiler_params=_params(("parallel", "parallel")),
        name="window_attn",
    )(sink, q, k, k, k, v, v, v, bias)


def _merge_kernel(x_ref, y_ref, a_ref, gt_ref, wfo_ref, wao_ref, wout_ref, g_ref, b_ref, o_ref):
    f = _dot(y_ref[...], wfo_ref[...])
    a = _dot(a_ref[...], wao_ref[...])
    merged = gt_ref[:, :D_MODEL].astype(F32) * f + gt_ref[:, D_MODEL:].astype(F32) * a
    o = _dot(merged.astype(BF16), wout_ref[...])
    o_ref[...] = _layer_norm(DEEPNORM_ALPHA * x_ref[...] + o, g_ref[...], b_ref[...])


def _merge(x, y, a, gt, wfo, wao, wout, g, b, tile):
    t = x.shape[0]
    row = lambda width: pl.BlockSpec((tile, width), lambda i: (i, 0))
    full = lambda arr: pl.BlockSpec(arr.shape, lambda i: (0,) * arr.ndim)
    return pl.pallas_call(
        _merge_kernel,
        grid=(t // tile,),
        in_specs=[row(D_MODEL), row(FOURIER_WIDTH), row(ATTN_WIDTH), row(GATE_WIDTH),
                  full(wfo), full(wao), full(wout), full(g), full(b)],
        out_specs=row(D_MODEL),
        out_shape=jax.ShapeDtypeStruct((t, D_MODEL), F32),
        compiler_params=_params(("parallel",)),
        name="merge_ln1",
    )(x, y, a, gt, wfo, wao, wout, g, b)


def _extract_sorted(s, ids, count, vals_ref):
    def body(r, carry):
        s, rank = carry
        m = jnp.max(s, axis=0, keepdims=True)
        first = jnp.min(jnp.where(s == m, ids, _BIG_INDEX), axis=0, keepdims=True)
        sel = ids == first
        if vals_ref is not None:
            vals_ref[pl.ds(r, 1), :] = m
        rank = jnp.where(sel, r.astype(F32), rank)
        s = jnp.where(sel, -jnp.inf, s)
        return s, rank

    rank0 = jnp.full(s.shape, float(count), F32)
    s, rank = lax.fori_loop(0, count, body, (s, rank0))
    return rank, s


def _select_kernel(x_ref, pq_ref, keys_ref, meta_ref, xt_ref, e1_ref, c1_ref, e2_ref, r2_ref,
                   q_scr, v1_scr, v2_scr, *, tile):
    x = x_ref[...]
    xt_ref[...] = x.T.astype(BF16)
    q = _dot(x.astype(BF16), pq_ref[...])
    for hp in range(2 * PEER_HEADS):
        q_scr[hp] = q[:, hp * PEER_KEY_DIM:(hp + 1) * PEER_KEY_DIM].astype(BF16)
    key_ids = lax.broadcasted_iota(jnp.int32, (PEER_NKEYS, LANES), 0).astype(F32)
    cand_ids = meta_ref[0]
    cand_mask = meta_ref[1]

    def head_body(h, carry):
        for c in range(tile // LANES):
            tok = slice(c * LANES, (c + 1) * LANES)
            s1 = _dot_nt(keys_ref[h, 0], q_scr[2 * h, tok, :])
            s2 = _dot_nt(keys_ref[h, 1], q_scr[2 * h + 1, tok, :])
            rank1, _ = _extract_sorted(s1, key_ids, PEER_TOPK, v1_scr)
            rank2, _ = _extract_sorted(s2, key_ids, PEER_TOPK, v2_scr)
            v1 = v1_scr[...]
            v2 = v2_scr[...]
            groups = [v1[r1:r1 + 1] + v2[lo:lo + SUBLANES] for r1, lo in _CAND_GROUPS]
            groups.append(v1[SUBLANES:] + v2[0:1])
            cand = jnp.concatenate(groups, axis=0) + cand_mask
            crank, _ = _extract_sorted(cand, cand_ids, PEER_TOPK, None)
            chosen = crank < float(PEER_TOPK)
            pexp = jnp.where(chosen, jnp.exp(cand - cand[0:1]), 0.0)
            inv_z = 1.0 / jnp.sum(pexp, axis=0, keepdims=True)
            cnt = chosen.astype(F32)
            counts = [jnp.sum(cnt[0:2 * SUBLANES], axis=0, keepdims=True)]
            for gi in range(2, len(_CAND_GROUPS)):
                counts.append(jnp.sum(cnt[gi * SUBLANES:(gi + 1) * SUBLANES], axis=0, keepdims=True))
            tail = cnt[len(_CAND_GROUPS) * SUBLANES:]
            c1 = jnp.zeros_like(s1)
            for r in range(PEER_TOPK):
                cr = counts[r] if r < SUBLANES else tail[r - SUBLANES:r - SUBLANES + 1]
                c1 = jnp.where(rank1 == float(r), cr, c1)
            in1 = rank1 < float(PEER_TOPK)
            in2 = rank2 < float(PEER_TOPK)
            e1_ref[h, :, tok] = jnp.where(in1, jnp.exp(s1 - v1[0:1]) * inv_z, 0.0)
            c1_ref[h, :, tok] = c1
            e2_ref[h, :, tok] = jnp.where(in2, jnp.exp(s2 - v2[0:1]), 0.0)
            r2_ref[h, :, tok] = rank2
        return carry

    lax.fori_loop(0, PEER_HEADS, head_body, 0)


def _cand_meta():
    ids = np.full((_CAND_ROWS, LANES), _BIG_INDEX, np.float32)
    mask = np.full((_CAND_ROWS, LANES), -np.inf, np.float32)
    for gi, (r1, lo) in enumerate(_CAND_GROUPS):
        for j in range(SUBLANES):
            r2 = lo + j
            if (r1 + 1) * (r2 + 1) <= PEER_TOPK:
                ids[gi * SUBLANES + j] = r1 * PEER_TOPK + r2
                mask[gi * SUBLANES + j] = 0.0
    base = len(_CAND_GROUPS) * SUBLANES
    for j in range(SUBLANES):
        ids[base + j] = (SUBLANES + j) * PEER_TOPK
        mask[base + j] = 0.0
    return np.stack([ids, mask])


def _peer_select(x, pq, keys, tile):
    t = x.shape[0]
    meta = jnp.asarray(_cand_meta())
    dense = jax.ShapeDtypeStruct((PEER_HEADS, PEER_NKEYS, t), F32)
    dense_spec = pl.BlockSpec((PEER_HEADS, PEER_NKEYS, tile), lambda i: (0, 0, i))
    full = lambda arr: pl.BlockSpec(arr.shape, lambda i: (0,) * arr.ndim)
    return pl.pallas_call(
        functools.partial(_select_kernel, tile=tile),
        grid=(t // tile,),
        in_specs=[pl.BlockSpec((tile, D_MODEL), lambda i: (i, 0)), full(pq), full(keys), full(meta)],
        out_specs=[pl.BlockSpec((D_MODEL, tile), lambda i: (0, i))] + [dense_spec] * 4,
        out_shape=[jax.ShapeDtypeStruct((D_MODEL, t), BF16)] + [dense] * 4,
        scratch_shapes=[
            pltpu.VMEM((2 * PEER_HEADS, tile, PEER_KEY_DIM), BF16),
            pltpu.VMEM((PEER_TOPK, LANES), F32),
            pltpu.VMEM((PEER_TOPK, LANES), F32),
        ],
        compiler_params=_params(("parallel",)),
        name="peer_select",
    )(x, pq, keys, meta)


def _dense_kernel(xt_ref, e1_ref, c1_ref, e2_ref, r2_ref, u_ref, vt_ref, o_ref, w_scr, *, rows_per_block):
    eb = pl.program_id(1)

    @pl.when(eb == 0)
    def _():
        o_ref[...] = jnp.zeros_like(o_ref)

    pre = _dot(u_ref[...], xt_ref[...])
    act = 0.5 * pre * (1.0 + lax.erf(pre * (1.0 / math.sqrt(2.0))))
    for jj in range(rows_per_block):
        j = eb * rows_per_block + jj
        gate = None
        for h in range(PEER_HEADS):
            c1 = c1_ref[h, pl.ds(j, 1), :]
            e1 = e1_ref[h, pl.ds(j, 1), :]
            term = jnp.where(r2_ref[h] < c1, e2_ref[h], 0.0) * e1
            gate = term if gate is None else gate + term
        rows = slice(jj * PEER_NKEYS, (jj + 1) * PEER_NKEYS)
        w_scr[rows, :] = (gate * act[rows]).astype(BF16)
    o_ref[...] += _dot(vt_ref[...], w_scr[...])


def _peer_dense(xt, e1, c1, e2, r2, u, vt, tile, eblock):
    t = xt.shape[1]
    dense_spec = pl.BlockSpec((PEER_HEADS, PEER_NKEYS, tile), lambda i, e: (0, 0, i))
    return pl.pallas_call(
        functools.partial(_dense_kernel, rows_per_block=eblock // PEER_NKEYS),
        grid=(t // tile, PEER_N_EXPERTS // eblock),
        in_specs=[pl.BlockSpec((D_MODEL, tile), lambda i, e: (0, i))] + [dense_spec] * 4 + [
            pl.BlockSpec((eblock, D_MODEL), lambda i, e: (e, 0)),
            pl.BlockSpec((D_MODEL, eblock), lambda i, e: (0, e)),
        ],
        out_specs=pl.BlockSpec((D_MODEL, tile), lambda i, e: (0, i)),
        out_shape=jax.ShapeDtypeStruct((D_MODEL, t), F32),
        scratch_shapes=[pltpu.VMEM((eblock, tile), BF16)],
        compiler_params=_params(("parallel", "arbitrary")),
        name="peer_dense",
    )(xt, e1, c1, e2, r2, u, vt)


def _final_kernel(x_ref, cmt_ref, p_ref, wg_ref, wp_ref, g_ref, b_ref, o_ref):
    x = x_ref[...]
    gate = jax.nn.sigmoid(_dot(x.astype(BF16), wg_ref[...]))
    ple = gate * _dot(p_ref[...].astype(BF16), wp_ref[...])
    o_ref[...] = _layer_norm(DEEPNORM_ALPHA * x + cmt_ref[...].T + ple, g_ref[...], b_ref[...])


def _final(x, cmt, p, wg, wp, g, b, tile):
    t = x.shape[0]
    full = lambda arr: pl.BlockSpec(arr.shape, lambda i: (0,) * arr.ndim)
    return pl.pallas_call(
        _final_kernel,
        grid=(t // tile,),
        in_specs=[pl.BlockSpec((tile, D_MODEL), lambda i: (i, 0)),
                  pl.BlockSpec((D_MODEL, tile), lambda i: (0, i)),
                  pl.BlockSpec((tile, PLE_DIM), lambda i: (i, 0)),
                  full(wg), full(wp), full(g), full(b)],
        out_specs=pl.BlockSpec((tile, D_MODEL), lambda i: (i, 0)),
        out_shape=jax.ShapeDtypeStruct((t, D_MODEL), F32),
        compiler_params=_params(("parallel",)),
        name="ple_ln2",
    )(x, cmt, p, wg, wp, g, b)


def _channel_dft():
    c = np.arange(FOURIER_GROUP_DIM)
    ang = 2.0 * np.pi * ((c[:, None] * c[None, :]) % FOURIER_GROUP_DIM) / FOURIER_GROUP_DIM
    eye = np.eye(N_FOURIER_GROUPS)
    return np.concatenate([np.kron(eye, np.cos(ang)), np.kron(eye, -np.sin(ang))], axis=1)


def _sequence_dft(s, scale):
    s_lo = 64
    s_hi = s // s_lo
    k = np.arange(s)
    ang_hi = 2.0 * np.pi * ((k[:, None] * np.arange(s_hi)[None, :]) % s_hi) / s_hi
    ang_lo = 2.0 * np.pi * ((k[:, None] * np.arange(s_lo)[None, :]) % s) / s
    ch = jnp.asarray(np.cos(ang_hi) * scale, F32)[:, :, None]
    sh = jnp.asarray(np.sin(ang_hi) * scale, F32)[:, :, None]
    cl = jnp.asarray(np.cos(ang_lo), F32)[:, None, :]
    sl = jnp.asarray(np.sin(ang_lo), F32)[:, None, :]
    cmat = (ch * cl - sh * sl).reshape(s, s).astype(BF16)
    smat = (sh * cl + ch * sl).reshape(s, s).astype(BF16)
    return cmat, smat


def _attn_bias():
    qi = np.arange(ATTN_BLOCK)
    kj = np.arange(3 * ATTN_BLOCK)
    rel = np.abs(qi[:, None] + ATTN_BLOCK - kj[None, :]).astype(np.float64)
    slopes = np.exp2(-8.0 * np.arange(1, N_HEADS + 1) / N_HEADS)
    bias = np.where(rel[None] <= WINDOW, -slopes[:, None, None] * rel[None], NEG_INF)
    return bias.astype(np.float32)


def _pick_tile(n, pref):
    tile = min(n, pref)
    assert n % tile == 0, (n, tile)
    return tile


def _trunk(x, p_all, consts, emb_ln, layers):
    bsz, s, _ = x.shape
    t = bsz * s
    mch, cmat, smat, bias = consts
    x = x.reshape(t, D_MODEL)
    row_tile = _pick_tile(t, 512)
    for i, lw in enumerate(layers):
        if i == 0:
            x, ab, q, k, v, gt = _inproj(x, emb_ln, lw["w_in"], mch, row_tile)
        else:
            ab, q, k, v, gt = _inproj(x, None, lw["w_in"], mch, row_tile)
        y = _seqdft(ab.reshape(bsz, s, 2 * FOURIER_WIDTH), cmat, smat, _pick_tile(s, 256))
        a = _attention(q.reshape(bsz, s, ATTN_WIDTH), k.reshape(bsz, s, KV_WIDTH),
                       v.reshape(bsz, s, KV_WIDTH), bias, lw["sink"])
        x = _merge(x, y.reshape(t, FOURIER_WIDTH), a.reshape(t, ATTN_WIDTH), gt,
                   lw["w_fo"], lw["w_ao"], lw["w_out"], lw["ln1_g"], lw["ln1_b"], row_tile)
        xt, e1, c1, e2, r2 = _peer_select(x, lw["pq"], lw["keys"], _pick_tile(t, 256))
        cmt = _peer_dense(xt, e1, c1, e2, r2, lw["u"], lw["vt"], _pick_tile(t, 512), 512)
        x = _final(x, cmt, p_all[i].reshape(t, PLE_DIM), lw["wg"], lw["wp"], lw["ln2_g"], lw["ln2_b"],
                   row_tile)
    return x.reshape(bsz, s, D_MODEL)


def _prepare_layer(i, w_in, attn_sink, w_fourier_out, w_attn_out, w_out, ln1_g, ln1_b, peer_w_q,
                   peer_keys, peer_u, peer_v, ple_w_gate, ple_w_proj, ln2_g, ln2_b):
    w = w_in[i]
    wq = w[:, _O_Q:_O_K].reshape(D_MODEL, N_KV_HEADS, Q_GROUP, HEAD_DIM).transpose(0, 2, 1, 3)
    wq = wq.reshape(D_MODEL, ATTN_WIDTH) * (HEAD_DIM ** -0.5)
    w_perm = jnp.concatenate([w[:, :_O_Q], wq, w[:, _O_K:]], axis=1).astype(BF16)
    wao = w_attn_out[i].reshape(N_KV_HEADS, Q_GROUP, HEAD_DIM, D_MODEL).transpose(1, 0, 2, 3)
    row = lambda a: a[i].reshape(1, D_MODEL).astype(F32)
    return dict(
        w_in=w_perm, sink=attn_sink[i].astype(F32),
        w_fo=w_fourier_out[i].astype(BF16), w_ao=wao.reshape(ATTN_WIDTH, D_MODEL).astype(BF16),
        w_out=w_out[i].astype(BF16), ln1_g=row(ln1_g), ln1_b=row(ln1_b),
        pq=peer_w_q[i].astype(BF16), keys=peer_keys[i].astype(BF16),
        u=peer_u[i].astype(BF16), vt=peer_v[i].astype(BF16).T,
        wg=ple_w_gate[i].astype(BF16), wp=ple_w_proj[i].astype(BF16), ln2_g=row(ln2_g), ln2_b=row(ln2_b),
    )


def kernel(x_prompt, x_sample, p_prompt, p_sample, emb_ln_g, emb_ln_b, w_in, attn_sink, w_fourier_out, w_attn_out, w_out, ln1_g, ln1_b, peer_w_q, peer_keys, peer_u, peer_v, ple_w_gate, ple_w_proj, ln2_g, ln2_b):
    depth = w_in.shape[0]
    layers = [_prepare_layer(i, w_in, attn_sink, w_fourier_out, w_attn_out, w_out, ln1_g, ln1_b, peer_w_q,
                             peer_keys, peer_u, peer_v, ple_w_gate, ple_w_proj, ln2_g, ln2_b)
              for i in range(depth)]
    emb_ln = (emb_ln_g.reshape(1, D_MODEL).astype(F32), emb_ln_b.reshape(1, D_MODEL).astype(F32))
    mch = jnp.asarray(_channel_dft() / math.sqrt(FOURIER_GROUP_DIM), BF16)
    bias = jnp.asarray(_attn_bias())
    outs = []
    for x, p in ((x_prompt, p_prompt), (x_sample, p_sample)):
        s = x.shape[1]
        cmat, smat = _sequence_dft(s, 1.0 / math.sqrt(s))
        outs.append(_trunk(x, p, (mch, cmat, smat, bias), emb_ln, layers))
    return tuple(outs)
```

```python
import functools
import math

import numpy as np
import jax
import jax.numpy as jnp
from jax import lax
from jax.experimental import pallas as pl
from jax.experimental.pallas import tpu as pltpu

F32 = jnp.float32
BF16 = jnp.bfloat16

D_MODEL = 1024
N_HEADS = 8
N_KV_HEADS = 2
Q_GROUP = N_HEADS // N_KV_HEADS
HEAD_DIM = 64
WINDOW = 128
ATTN_BLOCK = 128
ATTN_WIDTH = N_HEADS * HEAD_DIM
KV_WIDTH = N_KV_HEADS * HEAD_DIM
N_FOURIER_GROUPS = 8
FOURIER_GROUP_DIM = 64
FOURIER_WIDTH = N_FOURIER_GROUPS * FOURIER_GROUP_DIM
GATE_WIDTH = 2 * D_MODEL
IN_WIDTH = FOURIER_WIDTH + ATTN_WIDTH + 2 * KV_WIDTH + GATE_WIDTH
PEER_HEADS = 8
PEER_NKEYS = 128
PEER_N_EXPERTS = PEER_NKEYS * PEER_NKEYS
PEER_TOPK = 16
PEER_KEY_DIM = 128
PLE_DIM = 256
DEPTH = 2
DEEPNORM_ALPHA = (2 * DEPTH) ** 0.25
LN_EPS = 1e-5
NEG_INF = -1e30

LANES = 128
SUBLANES = 8
BF16_ROWS = 16
VMEM_LIMIT = 52 * 1024 * 1024

_O_F = 0
_O_Q = _O_F + FOURIER_WIDTH
_O_K = _O_Q + ATTN_WIDTH
_O_V = _O_K + KV_WIDTH
_O_G = _O_V + KV_WIDTH

_CAND_GROUPS = [(0, 0), (0, 8), (1, 0), (2, 0), (3, 0), (4, 0), (5, 0), (6, 0), (7, 0)]
_CAND_ROWS = 8 * (len(_CAND_GROUPS) + 1)
_BIG_INDEX = 1.0e6


def _params(semantics, flags=None):
    return pltpu.CompilerParams(dimension_semantics=semantics, vmem_limit_bytes=VMEM_LIMIT, flags=flags)


def _layer_norm(x, g, b):
    mu = jnp.mean(x, axis=-1, keepdims=True)
    xc = x - mu
    var = jnp.mean(xc * xc, axis=-1, keepdims=True)
    return xc * lax.rsqrt(var + LN_EPS) * g + b


def _dot(a, b):
    return jnp.dot(a, b, preferred_element_type=F32)


def _dot_nt(a, b):
    return lax.dot_general(a, b, (((1,), (1,)), ((), ())), preferred_element_type=F32)


def _inproj_kernel(*refs, apply_ln):
    if apply_ln:
        x_ref, g_ref, b_ref, w_ref, mch_ref, xn_ref, ab_ref, q_ref, k_ref, v_ref, gt_ref = refs
        x = _layer_norm(x_ref[...], g_ref[...], b_ref[...])
        xn_ref[...] = x
    else:
        x_ref, w_ref, mch_ref, ab_ref, q_ref, k_ref, v_ref, gt_ref = refs
        x = x_ref[...]
    xb = x.astype(BF16)
    f = _dot(xb, w_ref[:, _O_F:_O_Q])
    ab_ref[...] = _dot(f.astype(BF16), mch_ref[...]).astype(BF16)
    q_ref[...] = _dot(xb, w_ref[:, _O_Q:_O_K]).astype(BF16)
    k_ref[...] = _dot(xb, w_ref[:, _O_K:_O_V]).astype(BF16)
    v_ref[...] = _dot(xb, w_ref[:, _O_V:_O_G]).astype(BF16)
    gt_ref[...] = jax.nn.sigmoid(_dot(xb, w_ref[:, _O_G:IN_WIDTH])).astype(BF16)


def _inproj(x, ln, w_in, mch, tile):
    t = x.shape[0]
    apply_ln = ln is not None
    row = lambda width: pl.BlockSpec((tile, width), lambda i: (i, 0))
    full = lambda a: pl.BlockSpec(a.shape, lambda i: (0,) * a.ndim)
    ins = [x] + ([ln[0], ln[1]] if apply_ln else []) + [w_in, mch]
    in_specs = [row(D_MODEL)] + ([full(ln[0]), full(ln[1])] if apply_ln else []) + [full(w_in), full(mch)]
    widths = [2 * FOURIER_WIDTH, ATTN_WIDTH, KV_WIDTH, KV_WIDTH, GATE_WIDTH]
    out_shape = [jax.ShapeDtypeStruct((t, w), BF16) for w in widths]
    out_specs = [row(w) for w in widths]
    if apply_ln:
        out_shape = [jax.ShapeDtypeStruct((t, D_MODEL), F32)] + out_shape
        out_specs = [row(D_MODEL)] + out_specs
    return pl.pallas_call(
        functools.partial(_inproj_kernel, apply_ln=apply_ln),
        grid=(t // tile,),
        in_specs=in_specs,
        out_specs=out_specs,
        out_shape=out_shape,
        compiler_params=_params(("parallel",)),
        name="inproj_ln" if apply_ln else "inproj",
    )(*ins)


def _seqdft_kernel(c_ref, s_ref, a_ref, b_ref, y_ref):
    y_ref[...] = (_dot(c_ref[...], a_ref[...]) + _dot(s_ref[...], b_ref[...])).astype(BF16)


def _seqdft(ab, cmat, smat, tile):
    bsz, s, _ = ab.shape
    return pl.pallas_call(
        _seqdft_kernel,
        grid=(bsz, s // tile),
        in_specs=[
            pl.BlockSpec((tile, s), lambda b, m: (m, 0)),
            pl.BlockSpec((tile, s), lambda b, m: (m, 0)),
            pl.BlockSpec((None, s, FOURIER_WIDTH), lambda b, m: (b, 0, 0)),
            pl.BlockSpec((None, s, FOURIER_WIDTH), lambda b, m: (b, 0, 1)),
        ],
        out_specs=pl.BlockSpec((None, tile, FOURIER_WIDTH), lambda b, m: (b, m, 0)),
        out_shape=jax.ShapeDtypeStruct((bsz, s, FOURIER_WIDTH), BF16),
        compiler_params=_params(("parallel", "parallel")),
        name="seqdft",
    )(cmat, smat, ab, ab)


def _attn_kernel(sink_ref, q_ref, kp_ref, kc_ref, kn_ref, vp_ref, vc_ref, vn_ref, bias_ref, o_ref, *, nblocks):
    n = pl.program_id(1)
    col = lax.broadcasted_iota(jnp.int32, (1, 3 * ATTN_BLOCK), 1)
    off_edge = ((col < ATTN_BLOCK) & (n == 0)) | ((col >= 2 * ATTN_BLOCK) & (n == nblocks - 1))
    edge = jnp.where(off_edge, NEG_INF, 0.0).astype(F32)
    kcat = jnp.concatenate([kp_ref[...], kc_ref[...], kn_ref[...]], axis=0)
    vcat = jnp.concatenate([vp_ref[...], vc_ref[...], vn_ref[...]], axis=0)
    lane = lax.broadcasted_iota(jnp.int32, (ATTN_BLOCK, KV_WIDTH), 1)
    low = lane < HEAD_DIM
    for g in range(Q_GROUP):
        qg = q_ref[:, g * KV_WIDTH:(g + 1) * KV_WIDTH]
        halves = []
        for kh in range(N_KV_HEADS):
            h = kh * Q_GROUP + g
            qm = jnp.where(low if kh == 0 else jnp.logical_not(low), qg, jnp.zeros_like(qg))
            s = _dot_nt(qm, kcat) + bias_ref[h] + edge
            sink = sink_ref[h]
            m = jnp.maximum(jnp.max(s, axis=-1, keepdims=True), sink)
            p = jnp.exp(s - m)
            denom = jnp.sum(p, axis=-1, keepdims=True) + jnp.exp(sink - m)
            halves.append(_dot(p.astype(BF16), vcat) / denom)
        o_ref[:, g * KV_WIDTH:(g + 1) * KV_WIDTH] = jnp.where(low, halves[0], halves[1]).astype(BF16)


def _attention(q, k, v, bias, sink):
    bsz, s, _ = q.shape
    nb = s // ATTN_BLOCK
    kv_spec = lambda fn: pl.BlockSpec((None, ATTN_BLOCK, KV_WIDTH), fn)
    prev = lambda b, n: (b, jnp.maximum(n - 1, 0), 0)
    cur = lambda b, n: (b, n, 0)
    nxt = lambda b, n: (b, jnp.minimum(n + 1, nb - 1), 0)
    return pl.pallas_call(
        functools.partial(_attn_kernel, nblocks=nb),
        grid=(bsz, nb),
        in_specs=[
            pl.BlockSpec(memory_space=pltpu.SMEM),
            pl.BlockSpec((None, ATTN_BLOCK, ATTN_WIDTH), cur),
            kv_spec(prev), kv_spec(cur), kv_spec(nxt),
            kv_spec(prev), kv_spec(cur), kv_spec(nxt),
            pl.BlockSpec(bias.shape, lambda b, n: (0, 0, 0)),
        ],
        out_specs=pl.BlockSpec((None, ATTN_BLOCK, ATTN_WIDTH), cur),
        out_shape=jax.ShapeDtypeStruct((bsz, s, ATTN_WIDTH), BF16),
        compiler_params=_params(("parallel", "parallel")),
        name="window_attn",
    )(sink, q, k, k, k, v, v, v, bias)


def _merge_kernel(x_ref, y_ref, a_ref, gt_ref, wfo_ref, wao_ref, wout_ref, g_ref, b_ref, o_ref):
    f = _dot(y_ref[...], wfo_ref[...])
    a = _dot(a_ref[...], wao_ref[...])
    merged = gt_ref[:, :D_MODEL].astype(F32) * f + gt_ref[:, D_MODEL:].astype(F32) * a
    o = _dot(merged.astype(BF16), wout_ref[...])
    o_ref[...] = _layer_norm(DEEPNORM_ALPHA * x_ref[...] + o, g_ref[...], b_ref[...])


def _merge(x, y, a, gt, wfo, wao, wout, g, b, tile):
    t = x.shape[0]
    row = lambda width: pl.BlockSpec((tile, width), lambda i: (i, 0))
    full = lambda arr: pl.BlockSpec(arr.shape, lambda i: (0,) * arr.ndim)
    return pl.pallas_call(
        _merge_kernel,
        grid=(t // tile,),
        in_specs=[row(D_MODEL), row(FOURIER_WIDTH), row(ATTN_WIDTH), row(GATE_WIDTH),
                  full(wfo), full(wao), full(wout), full(g), full(b)],
        out_specs=row(D_MODEL),
        out_shape=jax.ShapeDtypeStruct((t, D_MODEL), F32),
        compiler_params=_params(("parallel",)),
        name="merge_ln1",
    )(x, y, a, gt, wfo, wao, wout, g, b)


def _extract_sorted(s, ids, count, vals_ref):
    def body(r, carry):
        s, rank = carry
        m = jnp.max(s, axis=0, keepdims=True)
        first = jnp.min(jnp.where(s == m, ids, _BIG_INDEX), axis=0, keepdims=True)
        sel = ids == first
        if vals_ref is not None:
            vals_ref[pl.ds(r, 1), :] = m
        rank = jnp.where(sel, r.astype(F32), rank)
        s = jnp.where(sel, -jnp.inf, s)
        return s, rank

    rank0 = jnp.full(s.shape, float(count), F32)
    s, rank = lax.fori_loop(0, count, body, (s, rank0))
    return rank, s


def _select_exact(s1, s2, key_ids, cand_ids, cand_mask, v1_scr, v2_scr):
    rank1, _ = _extract_sorted(s1, key_ids, PEER_TOPK, v1_scr)
    rank2, _ = _extract_sorted(s2, key_ids, PEER_TOPK, v2_scr)
    v1 = v1_scr[...]
    v2 = v2_scr[...]
    groups = [v1[r1:r1 + 1] + v2[lo:lo + SUBLANES] for r1, lo in _CAND_GROUPS]
    groups.append(v1[SUBLANES:] + v2[0:1])
    cand = jnp.concatenate(groups, axis=0) + cand_mask
    crank, _ = _extract_sorted(cand, cand_ids, PEER_TOPK, None)
    chosen = crank < float(PEER_TOPK)
    pexp = jnp.where(chosen, jnp.exp(cand - cand[0:1]), 0.0)
    inv_z = 1.0 / jnp.sum(pexp, axis=0, keepdims=True)
    cnt = chosen.astype(F32)
    counts = [jnp.sum(cnt[0:2 * SUBLANES], axis=0, keepdims=True)]
    for gi in range(2, len(_CAND_GROUPS)):
        counts.append(jnp.sum(cnt[gi * SUBLANES:(gi + 1) * SUBLANES], axis=0, keepdims=True))
    tail = cnt[len(_CAND_GROUPS) * SUBLANES:]
    c1 = jnp.zeros_like(s1)
    for r in range(PEER_TOPK):
        cr = counts[r] if r < SUBLANES else tail[r - SUBLANES:r - SUBLANES + 1]
        c1 = jnp.where(rank1 == float(r), cr, c1)
    e1 = jnp.exp(s1 - v1[0:1]) * inv_z
    e2 = jnp.exp(s2 - v2[0:1])
    return e1, c1, e2, rank2


def _merge_exchange_pairs(n):
    pairs = []
    t = max(1, math.ceil(math.log2(n)))
    p = 1 << (t - 1)
    while p > 0:
        q, r, d = 1 << (t - 1), 0, p
        while d > 0:
            pairs.extend((i, i + d) for i in range(n - d) if (i & p) == r)
            d, q, r = q - p, q >> 1, p
        p >>= 1
    return pairs


def _compare_exchange(v, i, j):
    v[i], v[j] = jnp.maximum(v[i], v[j]), jnp.minimum(v[i], v[j])


def _sort_desc(v):
    v = list(v)
    for i, j in _merge_exchange_pairs(len(v)):
        _compare_exchange(v, i, j)
    return v


def _bitonic_merge(v):
    v = list(v)
    d = len(v) // 2
    while d:
        for k in range(len(v)):
            if not k & d:
                _compare_exchange(v, k, k + d)
        d //= 2
    return v


def _across_sublanes(x, op):
    for shift in (4, 2, 1):
        x = op(x, pltpu.roll(x, shift, 0))
    return x


def _top16_of_sublane_lists(lists):
    for shift in (4, 2, 1):
        other = [pltpu.roll(x, shift, 0) for x in lists]
        n = len(lists)
        merged = []
        for k in range(PEER_TOPK):
            mine = lists[k] if k < n else None
            theirs = other[PEER_TOPK - 1 - k] if PEER_TOPK - 1 - k < n else None
            merged.append(mine if theirs is None else theirs if mine is None else jnp.maximum(mine, theirs))
        lists = _bitonic_merge(merged)
    return lists


def _select_sorted(s1, s2, cand_mask):
    nv = PEER_NKEYS // SUBLANES
    a1 = [s1[j * SUBLANES:(j + 1) * SUBLANES] for j in range(nv)]
    a2 = [s2[j * SUBLANES:(j + 1) * SUBLANES] for j in range(nv)]
    v1 = _top16_of_sublane_lists(_sort_desc(a1))
    v2 = _top16_of_sublane_lists(_sort_desc(a2))
    sub = lax.broadcasted_iota(jnp.int32, (SUBLANES, LANES), 0)

    def by_sublane(vals):
        out = vals[SUBLANES - 1]
        for j in range(SUBLANES - 2, -1, -1):
            out = jnp.where(sub == j, vals[j], out)
        return out

    v2_nat = {0: by_sublane(v2[:SUBLANES]), SUBLANES: by_sublane(v2[SUBLANES:])}
    groups = [v1[r1] + v2_nat[lo] for r1, lo in _CAND_GROUPS]
    groups.append(by_sublane(v1[SUBLANES:]) + v2[0])
    groups = [g + cand_mask[i * SUBLANES:(i + 1) * SUBLANES] for i, g in enumerate(groups)]
    top = _top16_of_sublane_lists(_sort_desc(groups))
    thr = top[PEER_TOPK - 1]
    z = None
    for k in range(PEER_TOPK):
        term = jnp.exp(top[k] - top[0])
        z = term if z is None else z + term
    inv_z = 1.0 / z
    picked = [jnp.where(g >= thr, 1.0, 0.0) for g in groups[:-1]]
    add = lambda a, b: a + b
    counts = [_across_sublanes(picked[0] + picked[1], add)]
    counts += [_across_sublanes(picked[r + 1], add) for r in range(1, SUBLANES)]
    counts += [jnp.where(v1[r] + v2[0] >= thr, 1.0, 0.0) for r in range(SUBLANES, PEER_TOPK)]
    e1, c1, e2, r2 = [], [], [], []
    c_total = None
    r_total = None
    for j in range(nv):
        c = jnp.zeros((SUBLANES, LANES), F32)
        r = jnp.full((SUBLANES, LANES), float(PEER_TOPK), F32)
        for k in range(PEER_TOPK - 1, -1, -1):
            c = jnp.where(a1[j] >= v1[k], counts[k], c)
            r = jnp.where(a2[j] >= v2[k], float(k), r)
        c1.append(c)
        r2.append(r)
        e1.append(jnp.exp(a1[j] - v1[0]) * inv_z)
        e2.append(jnp.exp(a2[j] - v2[0]))
        c_total = c if c_total is None else c_total + c
        r_total = (float(PEER_TOPK) - r) if r_total is None else r_total + (float(PEER_TOPK) - r)
    c_total = _across_sublanes(c_total, add)
    r_total = _across_sublanes(r_total, add)
    distinct_total = float(PEER_TOPK * (PEER_TOPK + 1) // 2)
    bad = jnp.where((c_total != float(PEER_TOPK)) | (r_total != distinct_total), 1.0, 0.0)
    cat = lambda parts: jnp.concatenate(parts, axis=0)
    return cat(e1), cat(c1), cat(e2), cat(r2), bad


def _select_kernel(x_ref, pq_ref, keys_ref, meta_ref, xt_ref, e1_ref, c1_ref, e2_ref, r2_ref,
                   q_scr, v1_scr, v2_scr, *, tile):
    x = x_ref[...]
    xt_ref[...] = x.T.astype(BF16)
    q = _dot(x.astype(BF16), pq_ref[...])
    for hp in range(2 * PEER_HEADS):
        q_scr[hp] = q[:, hp * PEER_KEY_DIM:(hp + 1) * PEER_KEY_DIM].astype(BF16)

    def head_body(h, carry):
        for c in range(tile // LANES):
            tok = slice(c * LANES, (c + 1) * LANES)
            s1 = _dot_nt(keys_ref[h, 0], q_scr[2 * h, tok, :])
            s2 = _dot_nt(keys_ref[h, 1], q_scr[2 * h + 1, tok, :])

            def emit(e1, c1, e2, r2):
                e1_ref[h, :, tok] = e1
                c1_ref[h, :, tok] = c1
                e2_ref[h, :, tok] = e2.astype(BF16)
                r2_ref[h, :, tok] = r2.astype(BF16)

            e1, c1, e2, r2, bad = _select_sorted(s1, s2, meta_ref[1])
            emit(e1, c1, e2, r2)

            @pl.when(jnp.max(bad) > 0.0)
            def _():
                key_ids = lax.broadcasted_iota(jnp.int32, (PEER_NKEYS, LANES), 0).astype(F32)
                emit(*_select_exact(s1, s2, key_ids, meta_ref[0], meta_ref[1], v1_scr, v2_scr))
        return carry

    lax.fori_loop(0, PEER_HEADS, head_body, 0)


def _cand_meta():
    ids = np.full((_CAND_ROWS, LANES), _BIG_INDEX, np.float32)
    mask = np.full((_CAND_ROWS, LANES), -np.inf, np.float32)
    for gi, (r1, lo) in enumerate(_CAND_GROUPS):
        for j in range(SUBLANES):
            r2 = lo + j
            if (r1 + 1) * (r2 + 1) <= PEER_TOPK:
                ids[gi * SUBLANES + j] = r1 * PEER_TOPK + r2
                mask[gi * SUBLANES + j] = 0.0
    base = len(_CAND_GROUPS) * SUBLANES
    for j in range(SUBLANES):
        ids[base + j] = (SUBLANES + j) * PEER_TOPK
        mask[base + j] = 0.0
    return np.stack([ids, mask])


def _peer_select(x, pq, keys, tile):
    t = x.shape[0]
    meta = jnp.asarray(_cand_meta())
    dense = lambda dtype: jax.ShapeDtypeStruct((PEER_HEADS, PEER_NKEYS, t), dtype)
    dense_spec = pl.BlockSpec((PEER_HEADS, PEER_NKEYS, tile), lambda i: (0, 0, i))
    full = lambda arr: pl.BlockSpec(arr.shape, lambda i: (0,) * arr.ndim)
    return pl.pallas_call(
        functools.partial(_select_kernel, tile=tile),
        grid=(t // tile,),
        in_specs=[pl.BlockSpec((tile, D_MODEL), lambda i: (i, 0)), full(pq), full(keys), full(meta)],
        out_specs=[pl.BlockSpec((D_MODEL, tile), lambda i: (0, i))] + [dense_spec] * 4,
        out_shape=[jax.ShapeDtypeStruct((D_MODEL, t), BF16), dense(F32), dense(F32), dense(BF16), dense(BF16)],
        scratch_shapes=[
            pltpu.VMEM((2 * PEER_HEADS, tile, PEER_KEY_DIM), BF16),
            pltpu.VMEM((PEER_TOPK, LANES), F32),
            pltpu.VMEM((PEER_TOPK, LANES), F32),
        ],
        compiler_params=_params(("parallel",)),
        name="peer_select",
    )(x, pq, keys, meta)


def _activations(xt_ref, u_ref, act_ref):
    pre = _dot(u_ref[...], xt_ref[...])
    act_ref[...] = (0.5 * pre * (1.0 + lax.erf(pre * (1.0 / math.sqrt(2.0))))).astype(BF16)


def _gate_activations(block, e1_ref, c1_ref, e2_ref, r2_ref, act_ref, w_ref, rows_per_block):
    tile = act_ref.shape[1]
    for jj in range(rows_per_block):
        j = block * rows_per_block + jj
        c1 = [jnp.broadcast_to(c1_ref[h, pl.ds(j, 1), :], (BF16_ROWS, tile)).astype(BF16)
              for h in range(PEER_HEADS)]
        e1 = [jnp.broadcast_to(e1_ref[h, pl.ds(j, 1), :], (BF16_ROWS, tile)).astype(BF16)
              for h in range(PEER_HEADS)]
        for g in range(PEER_NKEYS // BF16_ROWS):
            keys = slice(g * BF16_ROWS, (g + 1) * BF16_ROWS)
            gate = None
            for h in range(PEER_HEADS):
                picked = jnp.where(r2_ref[h, keys, :] < c1[h], e2_ref[h, keys, :], jnp.zeros((), BF16))
                term = picked * e1[h]
                gate = term if gate is None else gate + term
            rows = slice(jj * PEER_NKEYS + g * BF16_ROWS, jj * PEER_NKEYS + (g + 1) * BF16_ROWS)
            w_ref[rows, :] = gate * act_ref[rows, :]


def _dense_kernel(xt_ref, e1_ref, c1_ref, e2_ref, r2_ref, u_ref, vtp_ref, *refs, rows_per_block, nb):
    vt_refs, o_ref = refs[:nb], refs[nb]
    act_scr, w_scr = refs[nb + 1:2 * nb + 1], refs[2 * nb + 1:]
    e = pl.program_id(1)
    nrows = rows_per_block * PEER_NKEYS

    @pl.when(e == 0)
    def _():
        o_ref[...] = jnp.zeros_like(o_ref)
        w_scr[nb - 1][...] = jnp.zeros_like(w_scr[nb - 1])

    def activations(i):
        _activations(xt_ref, u_ref.at[i * nrows:(i + 1) * nrows, :], act_scr[i])

    activations(0)
    o_ref[...] += _dot(vtp_ref[...], w_scr[nb - 1][...])
    for i in range(nb):
        _gate_activations(nb * e + i, e1_ref, c1_ref, e2_ref, r2_ref, act_scr[i], w_scr[i], rows_per_block)
        if i + 1 < nb:
            activations(i + 1)
            o_ref[...] += _dot(vt_refs[i][...], w_scr[i][...])

    @pl.when(e == pl.num_programs(1) - 1)
    def _():
        o_ref[...] += _dot(vt_refs[nb - 1][...], w_scr[nb - 1][...])


def _peer_dense(xt, e1, c1, e2, r2, u, vt, tile, eblock, nb):
    t = xt.shape[1]
    row_spec = pl.BlockSpec((PEER_HEADS, PEER_NKEYS, tile), lambda i, e: (0, 0, i))
    vt_spec = lambda fn: pl.BlockSpec((D_MODEL, eblock), fn)
    return pl.pallas_call(
        functools.partial(_dense_kernel, rows_per_block=eblock // PEER_NKEYS, nb=nb),
        grid=(t // tile, PEER_N_EXPERTS // (nb * eblock)),
        in_specs=[pl.BlockSpec((D_MODEL, tile), lambda i, e: (0, i))] + [row_spec] * 4 + [
            pl.BlockSpec((nb * eblock, D_MODEL), lambda i, e: (e, 0)),
            vt_spec(lambda i, e: (0, jnp.maximum(nb * e - 1, 0))),
        ] + [vt_spec(functools.partial(lambda i, e, k: (0, nb * e + k), k=k)) for k in range(nb)],
        out_specs=pl.BlockSpec((D_MODEL, tile), lambda i, e: (0, i)),
        out_shape=jax.ShapeDtypeStruct((D_MODEL, t), F32),
        scratch_shapes=[pltpu.VMEM((eblock, tile), BF16)] * (2 * nb),
        compiler_params=_params(("parallel", "arbitrary")),
        name="peer_dense",
    )(xt, e1, c1, e2, r2, u, vt, *([vt] * nb))


def _final_kernel(x_ref, cmt_ref, p_ref, wg_ref, wp_ref, g_ref, b_ref, o_ref):
    x = x_ref[...]
    gate = jax.nn.sigmoid(_dot(x.astype(BF16), wg_ref[...]))
    ple = gate * _dot(p_ref[...].astype(BF16), wp_ref[...])
    o_ref[...] = _layer_norm(DEEPNORM_ALPHA * x + cmt_ref[...].T + ple, g_ref[...], b_ref[...])


def _final(x, cmt, p, wg, wp, g, b, tile):
    t = x.shape[0]
    full = lambda arr: pl.BlockSpec(arr.shape, lambda i: (0,) * arr.ndim)
    return pl.pallas_call(
        _final_kernel,
        grid=(t // tile,),
        in_specs=[pl.BlockSpec((tile, D_MODEL), lambda i: (i, 0)),
                  pl.BlockSpec((D_MODEL, tile), lambda i: (0, i)),
                  pl.BlockSpec((tile, PLE_DIM), lambda i: (i, 0)),
                  full(wg), full(wp), full(g), full(b)],
        out_specs=pl.BlockSpec((tile, D_MODEL), lambda i: (i, 0)),
        out_shape=jax.ShapeDtypeStruct((t, D_MODEL), F32),
        compiler_params=_params(("parallel",)),
        name="ple_ln2",
    )(x, cmt, p, wg, wp, g, b)


def _channel_dft():
    c = np.arange(FOURIER_GROUP_DIM)
    ang = 2.0 * np.pi * ((c[:, None] * c[None, :]) % FOURIER_GROUP_DIM) / FOURIER_GROUP_DIM
    eye = np.eye(N_FOURIER_GROUPS)
    return np.concatenate([np.kron(eye, np.cos(ang)), np.kron(eye, -np.sin(ang))], axis=1)


def _sequence_dft(s, scale):
    s_lo = 64
    s_hi = s // s_lo
    k = np.arange(s)
    ang_hi = 2.0 * np.pi * ((k[:, None] * np.arange(s_hi)[None, :]) % s_hi) / s_hi
    ang_lo = 2.0 * np.pi * ((k[:, None] * np.arange(s_lo)[None, :]) % s) / s
    ch = jnp.asarray(np.cos(ang_hi) * scale, F32)[:, :, None]
    sh = jnp.asarray(np.sin(ang_hi) * scale, F32)[:, :, None]
    cl = jnp.asarray(np.cos(ang_lo), F32)[:, None, :]
    sl = jnp.asarray(np.sin(ang_lo), F32)[:, None, :]
    cmat = (ch * cl - sh * sl).reshape(s, s).astype(BF16)
    smat = (sh * cl + ch * sl).reshape(s, s).astype(BF16)
    return cmat, smat


def _attn_bias():
    qi = np.arange(ATTN_BLOCK)
    kj = np.arange(3 * ATTN_BLOCK)
    rel = np.abs(qi[:, None] + ATTN_BLOCK - kj[None, :]).astype(np.float64)
    slopes = np.exp2(-8.0 * np.arange(1, N_HEADS + 1) / N_HEADS)
    bias = np.where(rel[None] <= WINDOW, -slopes[:, None, None] * rel[None], NEG_INF)
    return bias.astype(np.float32)


def _pick_tile(n, pref):
    tile = min(n, pref)
    assert n % tile == 0, (n, tile)
    return tile


def _trunk(x, p_all, consts, emb_ln, layers):
    bsz, s, _ = x.shape
    t = bsz * s
    mch, cmat, smat, bias = consts
    x = x.reshape(t, D_MODEL)
    row_tile = _pick_tile(t, 512)
    for i, lw in enumerate(layers):
        if i == 0:
            x, ab, q, k, v, gt = _inproj(x, emb_ln, lw["w_in"], mch, row_tile)
        else:
            ab, q, k, v, gt = _inproj(x, None, lw["w_in"], mch, row_tile)
        y = _seqdft(ab.reshape(bsz, s, 2 * FOURIER_WIDTH), cmat, smat, _pick_tile(s, 256))
        a = _attention(q.reshape(bsz, s, ATTN_WIDTH), k.reshape(bsz, s, KV_WIDTH),
                       v.reshape(bsz, s, KV_WIDTH), bias, lw["sink"])
        x = _merge(x, y.reshape(t, FOURIER_WIDTH), a.reshape(t, ATTN_WIDTH), gt,
                   lw["w_fo"], lw["w_ao"], lw["w_out"], lw["ln1_g"], lw["ln1_b"], row_tile)
        xt, e1, c1, e2, r2 = _peer_select(x, lw["pq"], lw["keys"], _pick_tile(t, 256))
        cmt = _peer_dense(xt, e1, c1, e2, r2, lw["u"], lw["vt"], _pick_tile(t, 512), 512, 2)
        x = _final(x, cmt, p_all[i].reshape(t, PLE_DIM), lw["wg"], lw["wp"], lw["ln2_g"], lw["ln2_b"],
                   row_tile)
    return x.reshape(bsz, s, D_MODEL)


def _prepare_layer(i, w_in, attn_sink, w_fourier_out, w_attn_out, w_out, ln1_g, ln1_b, peer_w_q,
                   peer_keys, peer_u, peer_v, ple_w_gate, ple_w_proj, ln2_g, ln2_b):
    w = w_in[i]
    wq = w[:, _O_Q:_O_K].reshape(D_MODEL, N_KV_HEADS, Q_GROUP, HEAD_DIM).transpose(0, 2, 1, 3)
    wq = wq.reshape(D_MODEL, ATTN_WIDTH) * (HEAD_DIM ** -0.5)
    w_perm = jnp.concatenate([w[:, :_O_Q], wq, w[:, _O_K:]], axis=1).astype(BF16)
    wao = w_attn_out[i].reshape(N_KV_HEADS, Q_GROUP, HEAD_DIM, D_MODEL).transpose(1, 0, 2, 3)
    row = lambda a: a[i].reshape(1, D_MODEL).astype(F32)
    return dict(
        w_in=w_perm, sink=attn_sink[i].astype(F32),
        w_fo=w_fourier_out[i].astype(BF16), w_ao=wao.reshape(ATTN_WIDTH, D_MODEL).astype(BF16),
        w_out=w_out[i].astype(BF16), ln1_g=row(ln1_g), ln1_b=row(ln1_b),
        pq=peer_w_q[i].astype(BF16), keys=peer_keys[i].astype(BF16),
        u=peer_u[i].astype(BF16), vt=peer_v[i].astype(BF16).T,
        wg=ple_w_gate[i].astype(BF16), wp=ple_w_proj[i].astype(BF16), ln2_g=row(ln2_g), ln2_b=row(ln2_b),
    )


def kernel(x_prompt, x_sample, p_prompt, p_sample, emb_ln_g, emb_ln_b, w_in, attn_sink, w_fourier_out, w_attn_out, w_out, ln1_g, ln1_b, peer_w_q, peer_keys, peer_u, peer_v, ple_w_gate, ple_w_proj, ln2_g, ln2_b):
    depth = w_in.shape[0]
    layers = [_prepare_layer(i, w_in, attn_sink, w_fourier_out, w_attn_out, w_out, ln1_g, ln1_b, peer_w_q,
                             peer_keys, peer_u, peer_v, ple_w_gate, ple_w_proj, ln2_g, ln2_b)
              for i in range(depth)]
    emb_ln = (emb_ln_g.reshape(1, D_MODEL).astype(F32), emb_ln_b.reshape(1, D_MODEL).astype(F32))
    mch = jnp.asarray(_channel_dft() / math.sqrt(FOURIER_GROUP_DIM), BF16)
    bias = jnp.asarray(_attn_bias())
    outs = []
    for x, p in ((x_prompt, p_prompt), (x_sample, p_sample)):
        s = x.shape[1]
        cmat, smat = _sequence_dft(s, 1.0 / math.sqrt(s))
        outs.append(_trunk(x, p, (mch, cmat, smat, bias), emb_ln, layers))
    return tuple(outs)
```

```python
import functools
import math

import numpy as np
import jax
import jax.numpy as jnp
from jax import lax
from jax.experimental import pallas as pl
from jax.experimental.pallas import tpu as pltpu

F32 = jnp.float32
BF16 = jnp.bfloat16

D_MODEL = 1024
N_HEADS = 8
N_KV_HEADS = 2
Q_GROUP = N_HEADS // N_KV_HEADS
HEAD_DIM = 64
WINDOW = 128
ATTN_BLOCK = 128
ATTN_WIDTH = N_HEADS * HEAD_DIM
KV_WIDTH = N_KV_HEADS * HEAD_DIM
N_FOURIER_GROUPS = 8
FOURIER_GROUP_DIM = 64
FOURIER_WIDTH = N_FOURIER_GROUPS * FOURIER_GROUP_DIM
GATE_WIDTH = 2 * D_MODEL
IN_WIDTH = FOURIER_WIDTH + ATTN_WIDTH + 2 * KV_WIDTH + GATE_WIDTH
PEER_HEADS = 8
PEER_NKEYS = 128
PEER_N_EXPERTS = PEER_NKEYS * PEER_NKEYS
PEER_TOPK = 16
PEER_KEY_DIM = 128
PLE_DIM = 256
DEPTH = 2
DEEPNORM_ALPHA = (2 * DEPTH) ** 0.25
LN_EPS = 1e-5
NEG_INF = -1e30

LANES = 128
SUBLANES = 8
BF16_ROWS = 16
VMEM_LIMIT = 52 * 1024 * 1024

_O_F = 0
_O_Q = _O_F + FOURIER_WIDTH
_O_K = _O_Q + ATTN_WIDTH
_O_V = _O_K + KV_WIDTH
_O_G = _O_V + KV_WIDTH

_CAND_GROUPS = [(0, 0), (0, 8), (1, 0), (2, 0), (3, 0), (4, 0), (5, 0), (6, 0), (7, 0)]
_CAND_ROWS = 8 * (len(_CAND_GROUPS) + 1)
_BIG_INDEX = 1.0e6


def _params(semantics, flags=None):
    return pltpu.CompilerParams(dimension_semantics=semantics, vmem_limit_bytes=VMEM_LIMIT, flags=flags)


def _layer_norm(x, g, b):
    mu = jnp.mean(x, axis=-1, keepdims=True)
    xc = x - mu
    var = jnp.mean(xc * xc, axis=-1, keepdims=True)
    return xc * lax.rsqrt(var + LN_EPS) * g + b


def _dot(a, b):
    return jnp.dot(a, b, preferred_element_type=F32)


def _dot_nt(a, b):
    return lax.dot_general(a, b, (((1,), (1,)), ((), ())), preferred_element_type=F32)


def _inproj_kernel(*refs, apply_ln):
    if apply_ln:
        x_ref, g_ref, b_ref, w_ref, mch_ref, xn_ref, ab_ref, q_ref, k_ref, v_ref, gt_ref = refs
        x = _layer_norm(x_ref[...], g_ref[...], b_ref[...])
        xn_ref[...] = x
    else:
        x_ref, w_ref, mch_ref, ab_ref, q_ref, k_ref, v_ref, gt_ref = refs
        x = x_ref[...]
    xb = x.astype(BF16)
    f = _dot(xb, w_ref[:, _O_F:_O_Q])
    ab_ref[...] = _dot(f.astype(BF16), mch_ref[...]).astype(BF16)
    q_ref[...] = _dot(xb, w_ref[:, _O_Q:_O_K]).astype(BF16)
    k_ref[...] = _dot(xb, w_ref[:, _O_K:_O_V]).astype(BF16)
    v_ref[...] = _dot(xb, w_ref[:, _O_V:_O_G]).astype(BF16)
    gt_ref[...] = jax.nn.sigmoid(_dot(xb, w_ref[:, _O_G:IN_WIDTH])).astype(BF16)


def _inproj(x, ln, w_in, mch, tile):
    t = x.shape[0]
    apply_ln = ln is not None
    row = lambda width: pl.BlockSpec((tile, width), lambda i: (i, 0))
    full = lambda a: pl.BlockSpec(a.shape, lambda i: (0,) * a.ndim)
    ins = [x] + ([ln[0], ln[1]] if apply_ln else []) + [w_in, mch]
    in_specs = [row(D_MODEL)] + ([full(ln[0]), full(ln[1])] if apply_ln else []) + [full(w_in), full(mch)]
    widths = [2 * FOURIER_WIDTH, ATTN_WIDTH, KV_WIDTH, KV_WIDTH, GATE_WIDTH]
    out_shape = [jax.ShapeDtypeStruct((t, w), BF16) for w in widths]
    out_specs = [row(w) for w in widths]
    if apply_ln:
        out_shape = [jax.ShapeDtypeStruct((t, D_MODEL), F32)] + out_shape
        out_specs = [row(D_MODEL)] + out_specs
    return pl.pallas_call(
        functools.partial(_inproj_kernel, apply_ln=apply_ln),
        grid=(t // tile,),
        in_specs=in_specs,
        out_specs=out_specs,
        out_shape=out_shape,
        compiler_params=_params(("parallel",)),
        name="inproj_ln" if apply_ln else "inproj",
    )(*ins)


def _seqdft_kernel(c_ref, s_ref, a_ref, b_ref, y_ref):
    y_ref[...] = (_dot(c_ref[...], a_ref[...]) + _dot(s_ref[...], b_ref[...])).astype(BF16)


def _seqdft(ab, cmat, smat, tile):
    bsz, s, _ = ab.shape
    return pl.pallas_call(
        _seqdft_kernel,
        grid=(bsz, s // tile),
        in_specs=[
            pl.BlockSpec((tile, s), lambda b, m: (m, 0)),
            pl.BlockSpec((tile, s), lambda b, m: (m, 0)),
            pl.BlockSpec((None, s, FOURIER_WIDTH), lambda b, m: (b, 0, 0)),
            pl.BlockSpec((None, s, FOURIER_WIDTH), lambda b, m: (b, 0, 1)),
        ],
        out_specs=pl.BlockSpec((None, tile, FOURIER_WIDTH), lambda b, m: (b, m, 0)),
        out_shape=jax.ShapeDtypeStruct((bsz, s, FOURIER_WIDTH), BF16),
        compiler_params=_params(("parallel", "parallel")),
        name="seqdft",
    )(cmat, smat, ab, ab)


def _attn_kernel(sink_ref, q_ref, kp_ref, kc_ref, kn_ref, vp_ref, vc_ref, vn_ref, bias_ref, o_ref, *, nblocks):
    n = pl.program_id(1)
    col = lax.broadcasted_iota(jnp.int32, (1, 3 * ATTN_BLOCK), 1)
    off_edge = ((col < ATTN_BLOCK) & (n == 0)) | ((col >= 2 * ATTN_BLOCK) & (n == nblocks - 1))
    edge = jnp.where(off_edge, NEG_INF, 0.0).astype(F32)
    kcat = jnp.concatenate([kp_ref[...], kc_ref[...], kn_ref[...]], axis=0)
    vcat = jnp.concatenate([vp_ref[...], vc_ref[...], vn_ref[...]], axis=0)
    vones = jnp.concatenate([vcat, jnp.ones_like(vcat)], axis=1)
    lane = lax.broadcasted_iota(jnp.int32, (ATTN_BLOCK, KV_WIDTH), 1)
    low = lane < HEAD_DIM
    for g in range(Q_GROUP):
        qg = q_ref[:, g * KV_WIDTH:(g + 1) * KV_WIDTH]
        halves = []
        for kh in range(N_KV_HEADS):
            h = kh * Q_GROUP + g
            qm = jnp.where(low if kh == 0 else jnp.logical_not(low), qg, jnp.zeros_like(qg))
            s = _dot_nt(qm, kcat) + bias_ref[h] + edge
            sink = sink_ref[h]
            m = jnp.maximum(jnp.max(s, axis=-1, keepdims=True), sink)
            p = jnp.exp(s - m)
            pv = _dot(p.astype(BF16), vones)
            halves.append(pv[:, :KV_WIDTH] / (pv[:, KV_WIDTH:] + jnp.exp(sink - m)))
        o_ref[:, g * KV_WIDTH:(g + 1) * KV_WIDTH] = jnp.where(low, halves[0], halves[1]).astype(BF16)


def _attention(q, k, v, bias, sink):
    bsz, s, _ = q.shape
    nb = s // ATTN_BLOCK
    kv_spec = lambda fn: pl.BlockSpec((None, ATTN_BLOCK, KV_WIDTH), fn)
    prev = lambda b, n: (b, jnp.maximum(n - 1, 0), 0)
    cur = lambda b, n: (b, n, 0)
    nxt = lambda b, n: (b, jnp.minimum(n + 1, nb - 1), 0)
    return pl.pallas_call(
        functools.partial(_attn_kernel, nblocks=nb),
        grid=(bsz, nb),
        in_specs=[
            pl.BlockSpec(memory_space=pltpu.SMEM),
            pl.BlockSpec((None, ATTN_BLOCK, ATTN_WIDTH), cur),
            kv_spec(prev), kv_spec(cur), kv_spec(nxt),
            kv_spec(prev), kv_spec(cur), kv_spec(nxt),
            pl.BlockSpec(bias.shape, lambda b, n: (0, 0, 0)),
        ],
        out_specs=pl.BlockSpec((None, ATTN_BLOCK, ATTN_WIDTH), cur),
        out_shape=jax.ShapeDtypeStruct((bsz, s, ATTN_WIDTH), BF16),
        compiler_params=_params(("parallel", "parallel")),
        name="window_attn",
    )(sink, q, k, k, k, v, v, v, bias)


def _merge_kernel(x_ref, y_ref, a_ref, gt_ref, wfo_ref, wao_ref, wout_ref, g_ref, b_ref, o_ref):
    f = _dot(y_ref[...], wfo_ref[...])
    a = _dot(a_ref[...], wao_ref[...])
    merged = gt_ref[:, :D_MODEL].astype(F32) * f + gt_ref[:, D_MODEL:].astype(F32) * a
    o = _dot(merged.astype(BF16), wout_ref[...])
    o_ref[...] = _layer_norm(DEEPNORM_ALPHA * x_ref[...] + o, g_ref[...], b_ref[...])


def _merge(x, y, a, gt, wfo, wao, wout, g, b, tile):
    t = x.shape[0]
    row = lambda width: pl.BlockSpec((tile, width), lambda i: (i, 0))
    full = lambda arr: pl.BlockSpec(arr.shape, lambda i: (0,) * arr.ndim)
    return pl.pallas_call(
        _merge_kernel,
        grid=(t // tile,),
        in_specs=[row(D_MODEL), row(FOURIER_WIDTH), row(ATTN_WIDTH), row(GATE_WIDTH),
                  full(wfo), full(wao), full(wout), full(g), full(b)],
        out_specs=row(D_MODEL),
        out_shape=jax.ShapeDtypeStruct((t, D_MODEL), F32),
        compiler_params=_params(("parallel",)),
        name="merge_ln1",
    )(x, y, a, gt, wfo, wao, wout, g, b)


def _extract_sorted(s, ids, count, vals_ref):
    def body(r, carry):
        s, rank = carry
        m = jnp.max(s, axis=0, keepdims=True)
        first = jnp.min(jnp.where(s == m, ids, _BIG_INDEX), axis=0, keepdims=True)
        sel = ids == first
        if vals_ref is not None:
            vals_ref[pl.ds(r, 1), :] = m
        rank = jnp.where(sel, lax.convert_element_type(r, F32), rank)
        s = jnp.where(sel, -jnp.inf, s)
        return s, rank

    rank0 = jnp.full(s.shape, float(count), F32)
    s, rank = lax.fori_loop(0, count, body, (s, rank0))
    return rank, s


def _select_exact(s1, s2, key_ids, cand_ids, cand_mask, v1_scr, v2_scr):
    rank1, _ = _extract_sorted(s1, key_ids, PEER_TOPK, v1_scr)
    rank2, _ = _extract_sorted(s2, key_ids, PEER_TOPK, v2_scr)
    v1 = v1_scr[...]
    v2 = v2_scr[...]
    groups = [v1[r1:r1 + 1] + v2[lo:lo + SUBLANES] for r1, lo in _CAND_GROUPS]
    groups.append(v1[SUBLANES:] + v2[0:1])
    cand = jnp.concatenate(groups, axis=0) + cand_mask
    crank, _ = _extract_sorted(cand, cand_ids, PEER_TOPK, None)
    chosen = crank < float(PEER_TOPK)
    pexp = jnp.where(chosen, jnp.exp(cand - cand[0:1]), 0.0)
    inv_z = 1.0 / jnp.sum(pexp, axis=0, keepdims=True)
    cnt = chosen.astype(F32)
    counts = [jnp.sum(cnt[0:2 * SUBLANES], axis=0, keepdims=True)]
    for gi in range(2, len(_CAND_GROUPS)):
        counts.append(jnp.sum(cnt[gi * SUBLANES:(gi + 1) * SUBLANES], axis=0, keepdims=True))
    tail = cnt[len(_CAND_GROUPS) * SUBLANES:]
    c1 = jnp.zeros_like(s1)
    for r in range(PEER_TOPK):
        cr = counts[r] if r < SUBLANES else tail[r - SUBLANES:r - SUBLANES + 1]
        c1 = jnp.where(rank1 == float(r), cr, c1)
    e1 = jnp.exp(s1 - v1[0:1]) * inv_z
    e2 = jnp.exp(s2 - v2[0:1])
    return e1, c1, e2, rank2


def _merge_exchange_pairs(n):
    pairs = []
    t = max(1, math.ceil(math.log2(n)))
    p = 1 << (t - 1)
    while p > 0:
        q, r, d = 1 << (t - 1), 0, p
        while d > 0:
            pairs.extend((i, i + d) for i in range(n - d) if (i & p) == r)
            d, q, r = q - p, q >> 1, p
        p >>= 1
    return pairs


def _compare_exchange(v, i, j):
    v[i], v[j] = jnp.maximum(v[i], v[j]), jnp.minimum(v[i], v[j])


def _sort_desc(v):
    v = list(v)
    for i, j in _merge_exchange_pairs(len(v)):
        _compare_exchange(v, i, j)
    return v


def _bitonic_merge(v):
    v = list(v)
    d = len(v) // 2
    while d:
        for k in range(len(v)):
            if not k & d:
                _compare_exchange(v, k, k + d)
        d //= 2
    return v


def _across_sublanes(x, op):
    for shift in (4, 2, 1):
        x = op(x, pltpu.roll(x, shift, 0))
    return x


def _top16_of_sublane_lists(lists):
    for shift in (4, 2, 1):
        other = [pltpu.roll(x, shift, 0) for x in lists]
        n = len(lists)
        merged = []
        for k in range(PEER_TOPK):
            mine = lists[k] if k < n else None
            theirs = other[PEER_TOPK - 1 - k] if PEER_TOPK - 1 - k < n else None
            merged.append(mine if theirs is None else theirs if mine is None else jnp.maximum(mine, theirs))
        lists = _bitonic_merge(merged)
    return lists


def _select_sorted(s1, s2, cand_mask):
    nv = PEER_NKEYS // SUBLANES
    a1 = [s1[j * SUBLANES:(j + 1) * SUBLANES] for j in range(nv)]
    a2 = [s2[j * SUBLANES:(j + 1) * SUBLANES] for j in range(nv)]
    v1 = _top16_of_sublane_lists(_sort_desc(a1))
    v2 = _top16_of_sublane_lists(_sort_desc(a2))
    sub = lax.broadcasted_iota(jnp.int32, (SUBLANES, LANES), 0)

    def by_sublane(vals):
        out = vals[SUBLANES - 1]
        for j in range(SUBLANES - 2, -1, -1):
            out = jnp.where(sub == j, vals[j], out)
        return out

    v2_nat = {0: by_sublane(v2[:SUBLANES]), SUBLANES: by_sublane(v2[SUBLANES:])}
    groups = [v1[r1] + v2_nat[lo] for r1, lo in _CAND_GROUPS]
    groups.append(by_sublane(v1[SUBLANES:]) + v2[0])
    groups = [g + cand_mask[i * SUBLANES:(i + 1) * SUBLANES] for i, g in enumerate(groups)]
    top = _top16_of_sublane_lists(_sort_desc(groups))
    thr = top[PEER_TOPK - 1]
    z = None
    for k in range(PEER_TOPK):
        term = jnp.exp(top[k] - top[0])
        z = term if z is None else z + term
    inv_z = 1.0 / z
    picked = [jnp.where(g >= thr, 1.0, 0.0) for g in groups[:-1]]
    add = lambda a, b: a + b
    counts = [_across_sublanes(picked[0] + picked[1], add)]
    counts += [_across_sublanes(picked[r + 1], add) for r in range(1, SUBLANES)]
    counts += [jnp.where(v1[r] + v2[0] >= thr, 1.0, 0.0) for r in range(SUBLANES, PEER_TOPK)]
    e1, c1, e2, r2 = [], [], [], []
    c_total = None
    r_total = None
    for j in range(nv):
        c = jnp.zeros((SUBLANES, LANES), F32)
        r = jnp.full((SUBLANES, LANES), float(PEER_TOPK), F32)
        for k in range(PEER_TOPK - 1, -1, -1):
            c = jnp.where(a1[j] >= v1[k], counts[k], c)
            r = jnp.where(a2[j] >= v2[k], float(k), r)
        c1.append(c)
        r2.append(r)
        e1.append(jnp.exp(a1[j] - v1[0]) * inv_z)
        e2.append(jnp.exp(a2[j] - v2[0]))
        c_total = c if c_total is None else c_total + c
        r_total = (float(PEER_TOPK) - r) if r_total is None else r_total + (float(PEER_TOPK) - r)
    c_total = _across_sublanes(c_total, add)
    r_total = _across_sublanes(r_total, add)
    distinct_total = float(PEER_TOPK * (PEER_TOPK + 1) // 2)
    bad = jnp.where((c_total != float(PEER_TOPK)) | (r_total != distinct_total), 1.0, 0.0)
    cat = lambda parts: jnp.concatenate(parts, axis=0)
    return cat(e1), cat(c1), cat(e2), cat(r2), bad


def _select_kernel(x_ref, pq_ref, keys_ref, meta_ref, xt_ref, e1_ref, c1_ref, e2_ref, r2_ref,
                   q_scr, v1_scr, v2_scr, *, tile):
    x = x_ref[...]
    xt_ref[...] = x.T.astype(BF16)
    q = _dot(x.astype(BF16), pq_ref[...])
    for hp in range(2 * PEER_HEADS):
        q_scr[hp] = q[:, hp * PEER_KEY_DIM:(hp + 1) * PEER_KEY_DIM].astype(BF16)

    def head_body(h, carry):
        for c in range(tile // LANES):
            tok = slice(c * LANES, (c + 1) * LANES)
            s1 = _dot_nt(keys_ref[h, 0], q_scr[2 * h, tok, :])
            s2 = _dot_nt(keys_ref[h, 1], q_scr[2 * h + 1, tok, :])

            def emit(e1, c1, e2, r2):
                e1_ref[h, :, tok] = e1
                c1_ref[h, :, tok] = c1
                e2_ref[h, :, tok] = e2.astype(BF16)
                r2_ref[h, :, tok] = r2.astype(BF16)

            e1, c1, e2, r2, bad = _select_sorted(s1, s2, meta_ref[1])
            emit(e1, c1, e2, r2)

            @pl.when(jnp.max(bad) > 0.0)
            def _():
                key_ids = lax.broadcasted_iota(jnp.int32, (PEER_NKEYS, LANES), 0).astype(F32)
                emit(*_select_exact(s1, s2, key_ids, meta_ref[0], meta_ref[1], v1_scr, v2_scr))
        return carry

    lax.fori_loop(0, PEER_HEADS, head_body, 0)


def _cand_meta():
    ids = np.full((_CAND_ROWS, LANES), _BIG_INDEX, np.float32)
    mask = np.full((_CAND_ROWS, LANES), -np.inf, np.float32)
    for gi, (r1, lo) in enumerate(_CAND_GROUPS):
        for j in range(SUBLANES):
            r2 = lo + j
            if (r1 + 1) * (r2 + 1) <= PEER_TOPK:
                ids[gi * SUBLANES + j] = r1 * PEER_TOPK + r2
                mask[gi * SUBLANES + j] = 0.0
    base = len(_CAND_GROUPS) * SUBLANES
    for j in range(SUBLANES):
        ids[base + j] = (SUBLANES + j) * PEER_TOPK
        mask[base + j] = 0.0
    return np.stack([ids, mask])


def _peer_select(x, pq, keys, tile):
    t = x.shape[0]
    meta = jnp.asarray(_cand_meta())
    dense = lambda dtype: jax.ShapeDtypeStruct((PEER_HEADS, PEER_NKEYS, t), dtype)
    dense_spec = pl.BlockSpec((PEER_HEADS, PEER_NKEYS, tile), lambda i: (0, 0, i))
    full = lambda arr: pl.BlockSpec(arr.shape, lambda i: (0,) * arr.ndim)
    return pl.pallas_call(
        functools.partial(_select_kernel, tile=tile),
        grid=(t // tile,),
        in_specs=[pl.BlockSpec((tile, D_MODEL), lambda i: (i, 0)), full(pq), full(keys), full(meta)],
        out_specs=[pl.BlockSpec((D_MODEL, tile), lambda i: (0, i))] + [dense_spec] * 4,
        out_shape=[jax.ShapeDtypeStruct((D_MODEL, t), BF16), dense(F32), dense(F32), dense(BF16), dense(BF16)],
        scratch_shapes=[
            pltpu.VMEM((2 * PEER_HEADS, tile, PEER_KEY_DIM), BF16),
            pltpu.VMEM((PEER_TOPK, LANES), F32),
            pltpu.VMEM((PEER_TOPK, LANES), F32),
        ],
        compiler_params=_params(("parallel",)),
        name="peer_select",
    )(x, pq, keys, meta)


def _activations(xt_ref, u_ref, act_ref):
    pre = _dot(u_ref[...], xt_ref[...])
    act_ref[...] = (0.5 * pre * (1.0 + lax.erf(pre * (1.0 / math.sqrt(2.0))))).astype(BF16)


def _gate_activations(block, e1_ref, c1_ref, e2_ref, r2_ref, act_ref, w_ref, rows_per_block):
    tile = act_ref.shape[1]
    for jj in range(rows_per_block):
        j = block * rows_per_block + jj
        c1 = [jnp.broadcast_to(c1_ref[h, pl.ds(j, 1), :], (BF16_ROWS, tile)).astype(BF16)
              for h in range(PEER_HEADS)]
        e1 = [jnp.broadcast_to(e1_ref[h, pl.ds(j, 1), :], (BF16_ROWS, tile)).astype(BF16)
              for h in range(PEER_HEADS)]
        for g in range(PEER_NKEYS // BF16_ROWS):
            keys = slice(g * BF16_ROWS, (g + 1) * BF16_ROWS)
            gate = None
            for h in range(PEER_HEADS):
                picked = jnp.where(r2_ref[h, keys, :] < c1[h], e2_ref[h, keys, :], jnp.zeros((), BF16))
                term = picked * e1[h]
                gate = term if gate is None else gate + term
            rows = slice(jj * PEER_NKEYS + g * BF16_ROWS, jj * PEER_NKEYS + (g + 1) * BF16_ROWS)
            w_ref[rows, :] = gate * act_ref[rows, :]


def _dense_kernel(xt_ref, e1_ref, c1_ref, e2_ref, r2_ref, u_ref, vtp_ref, *refs, rows_per_block, nb):
    vt_refs, o_ref = refs[:nb], refs[nb]
    act_scr, w_scr = refs[nb + 1:2 * nb + 1], refs[2 * nb + 1:]
    e = pl.program_id(1)
    nrows = rows_per_block * PEER_NKEYS

    @pl.when(e == 0)
    def _():
        o_ref[...] = jnp.zeros_like(o_ref)
        w_scr[nb - 1][...] = jnp.zeros_like(w_scr[nb - 1])

    def activations(i):
        _activations(xt_ref, u_ref.at[i * nrows:(i + 1) * nrows, :], act_scr[i])

    activations(0)
    o_ref[...] += _dot(vtp_ref[...], w_scr[nb - 1][...])
    for i in range(nb):
        _gate_activations(nb * e + i, e1_ref, c1_ref, e2_ref, r2_ref, act_scr[i], w_scr[i], rows_per_block)
        if i + 1 < nb:
            activations(i + 1)
            o_ref[...] += _dot(vt_refs[i][...], w_scr[i][...])

    @pl.when(e == pl.num_programs(1) - 1)
    def _():
        o_ref[...] += _dot(vt_refs[nb - 1][...], w_scr[nb - 1][...])


def _peer_dense(xt, e1, c1, e2, r2, u, vt, tile, eblock, nb):
    t = xt.shape[1]
    row_spec = pl.BlockSpec((PEER_HEADS, PEER_NKEYS, tile), lambda i, e: (0, 0, i))
    vt_spec = lambda fn: pl.BlockSpec((D_MODEL, eblock), fn)
    return pl.pallas_call(
        functools.partial(_dense_kernel, rows_per_block=eblock // PEER_NKEYS, nb=nb),
        grid=(t // tile, PEER_N_EXPERTS // (nb * eblock)),
        in_specs=[pl.BlockSpec((D_MODEL, tile), lambda i, e: (0, i))] + [row_spec] * 4 + [
            pl.BlockSpec((nb * eblock, D_MODEL), lambda i, e: (e, 0)),
            vt_spec(lambda i, e: (0, jnp.maximum(nb * e - 1, 0))),
        ] + [vt_spec(functools.partial(lambda i, e, k: (0, nb * e + k), k=k)) for k in range(nb)],
        out_specs=pl.BlockSpec((D_MODEL, tile), lambda i, e: (0, i)),
        out_shape=jax.ShapeDtypeStruct((D_MODEL, t), F32),
        scratch_shapes=[pltpu.VMEM((eblock, tile), BF16)] * (2 * nb),
        compiler_params=_params(("parallel", "arbitrary")),
        name="peer_dense",
    )(xt, e1, c1, e2, r2, u, vt, *([vt] * nb))


def _final_kernel(x_ref, cmt_ref, p_ref, wg_ref, wp_ref, g_ref, b_ref, o_ref):
    x = x_ref[...]
    gate = jax.nn.sigmoid(_dot(x.astype(BF16), wg_ref[...]))
    ple = gate * _dot(p_ref[...].astype(BF16), wp_ref[...])
    o_ref[...] = _layer_norm(DEEPNORM_ALPHA * x + cmt_ref[...].T + ple, g_ref[...], b_ref[...])


def _final(x, cmt, p, wg, wp, g, b, tile):
    t = x.shape[0]
    full = lambda arr: pl.BlockSpec(arr.shape, lambda i: (0,) * arr.ndim)
    return pl.pallas_call(
        _final_kernel,
        grid=(t // tile,),
        in_specs=[pl.BlockSpec((tile, D_MODEL), lambda i: (i, 0)),
                  pl.BlockSpec((D_MODEL, tile), lambda i: (0, i)),
                  pl.BlockSpec((tile, PLE_DIM), lambda i: (i, 0)),
                  full(wg), full(wp), full(g), full(b)],
        out_specs=pl.BlockSpec((tile, D_MODEL), lambda i: (i, 0)),
        out_shape=jax.ShapeDtypeStruct((t, D_MODEL), F32),
        compiler_params=_params(("parallel",)),
        name="ple_ln2",
    )(x, cmt, p, wg, wp, g, b)


def _channel_dft():
    c = np.arange(FOURIER_GROUP_DIM)
    ang = 2.0 * np.pi * ((c[:, None] * c[None, :]) % FOURIER_GROUP_DIM) / FOURIER_GROUP_DIM
    eye = np.eye(N_FOURIER_GROUPS)
    return np.concatenate([np.kron(eye, np.cos(ang)), np.kron(eye, -np.sin(ang))], axis=1)


def _sequence_dft(s, scale):
    s_lo = 64
    s_hi = s // s_lo
    k = np.arange(s)
    ang_hi = 2.0 * np.pi * ((k[:, None] * np.arange(s_hi)[None, :]) % s_hi) / s_hi
    ang_lo = 2.0 * np.pi * ((k[:, None] * np.arange(s_lo)[None, :]) % s) / s
    ch = jnp.asarray(np.cos(ang_hi) * scale, F32)[:, :, None]
    sh = jnp.asarray(np.sin(ang_hi) * scale, F32)[:, :, None]
    cl = jnp.asarray(np.cos(ang_lo), F32)[:, None, :]
    sl = jnp.asarray(np.sin(ang_lo), F32)[:, None, :]
    cmat = (ch * cl - sh * sl).reshape(s, s).astype(BF16)
    smat = (sh * cl + ch * sl).reshape(s, s).astype(BF16)
    return cmat, smat


def _attn_bias():
    qi = np.arange(ATTN_BLOCK)
    kj = np.arange(3 * ATTN_BLOCK)
    rel = np.abs(qi[:, None] + ATTN_BLOCK - kj[None, :]).astype(np.float64)
    slopes = np.exp2(-8.0 * np.arange(1, N_HEADS + 1) / N_HEADS)
    bias = np.where(rel[None] <= WINDOW, -slopes[:, None, None] * rel[None], NEG_INF)
    return bias.astype(np.float32)


def _pick_tile(n, pref):
    tile = min(n, pref)
    assert n % tile == 0, (n, tile)
    return tile


def _trunk(x, p_all, consts, emb_ln, layers):
    bsz, s, _ = x.shape
    t = bsz * s
    mch, cmat, smat, bias = consts
    x = x.reshape(t, D_MODEL)
    row_tile = _pick_tile(t, 512)
    for i, lw in enumerate(layers):
        if i == 0:
            x, ab, q, k, v, gt = _inproj(x, emb_ln, lw["w_in"], mch, row_tile)
        else:
            ab, q, k, v, gt = _inproj(x, None, lw["w_in"], mch, row_tile)
        y = _seqdft(ab.reshape(bsz, s, 2 * FOURIER_WIDTH), cmat, smat, _pick_tile(s, 256))
        a = _attention(q.reshape(bsz, s, ATTN_WIDTH), k.reshape(bsz, s, KV_WIDTH),
                       v.reshape(bsz, s, KV_WIDTH), bias, lw["sink"])
        x = _merge(x, y.reshape(t, FOURIER_WIDTH), a.reshape(t, ATTN_WIDTH), gt,
                   lw["w_fo"], lw["w_ao"], lw["w_out"], lw["ln1_g"], lw["ln1_b"], row_tile)
        xt, e1, c1, e2, r2 = _peer_select(x, lw["pq"], lw["keys"], _pick_tile(t, 256))
        cmt = _peer_dense(xt, e1, c1, e2, r2, lw["u"], lw["vt"], _pick_tile(t, 512), 512, 2)
        x = _final(x, cmt, p_all[i].reshape(t, PLE_DIM), lw["wg"], lw["wp"], lw["ln2_g"], lw["ln2_b"],
                   row_tile)
    return x.reshape(bsz, s, D_MODEL)


def _prepare_layer(i, w_in, attn_sink, w_fourier_out, w_attn_out, w_out, ln1_g, ln1_b, peer_w_q,
                   peer_keys, peer_u, peer_v, ple_w_gate, ple_w_proj, ln2_g, ln2_b):
    w = w_in[i]
    wq = w[:, _O_Q:_O_K].reshape(D_MODEL, N_KV_HEADS, Q_GROUP, HEAD_DIM).transpose(0, 2, 1, 3)
    wq = wq.reshape(D_MODEL, ATTN_WIDTH) * (HEAD_DIM ** -0.5)
    w_perm = jnp.concatenate([w[:, :_O_Q], wq, w[:, _O_K:]], axis=1).astype(BF16)
    wao = w_attn_out[i].reshape(N_KV_HEADS, Q_GROUP, HEAD_DIM, D_MODEL).transpose(1, 0, 2, 3)
    row = lambda a: a[i].reshape(1, D_MODEL).astype(F32)
    return dict(
        w_in=w_perm, sink=attn_sink[i].astype(F32),
        w_fo=w_fourier_out[i].astype(BF16), w_ao=wao.reshape(ATTN_WIDTH, D_MODEL).astype(BF16),
        w_out=w_out[i].astype(BF16), ln1_g=row(ln1_g), ln1_b=row(ln1_b),
        pq=peer_w_q[i].astype(BF16), keys=peer_keys[i].astype(BF16),
        u=peer_u[i].astype(BF16), vt=peer_v[i].astype(BF16).T,
        wg=ple_w_gate[i].astype(BF16), wp=ple_w_proj[i].astype(BF16), ln2_g=row(ln2_g), ln2_b=row(ln2_b),
    )


def kernel(x_prompt, x_sample, p_prompt, p_sample, emb_ln_g, emb_ln_b, w_in, attn_sink, w_fourier_out, w_attn_out, w_out, ln1_g, ln1_b, peer_w_q, peer_keys, peer_u, peer_v, ple_w_gate, ple_w_proj, ln2_g, ln2_b):
    depth = w_in.shape[0]
    layers = [_prepare_layer(i, w_in, attn_sink, w_fourier_out, w_attn_out, w_out, ln1_g, ln1_b, peer_w_q,
                             peer_keys, peer_u, peer_v, ple_w_gate, ple_w_proj, ln2_g, ln2_b)
              for i in range(depth)]
    emb_ln = (emb_ln_g.reshape(1, D_MODEL).astype(F32), emb_ln_b.reshape(1, D_MODEL).astype(F32))
    mch = jnp.asarray(_channel_dft() / math.sqrt(FOURIER_GROUP_DIM), BF16)
    bias = jnp.asarray(_attn_bias())
    outs = []
    for x, p in ((x_prompt, p_prompt), (x_sample, p_sample)):
        s = x.shape[1]
        cmat, smat = _sequence_dft(s, 1.0 / math.sqrt(s))
        outs.append(_trunk(x, p, (mch, cmat, smat, bias), emb_ln, layers))
    return tuple(outs)
```

```python
import functools
import math

import numpy as np
import jax
import jax.numpy as jnp
from jax import lax
from jax.experimental import pallas as pl
from jax.experimental.pallas import tpu as pltpu

F32 = jnp.float32
BF16 = jnp.bfloat16

D_MODEL = 1024
N_HEADS = 8
N_KV_HEADS = 2
Q_GROUP = N_HEADS // N_KV_HEADS
HEAD_DIM = 64
WINDOW = 128
ATTN_BLOCK = 128
ATTN_WIDTH = N_HEADS * HEAD_DIM
KV_WIDTH = N_KV_HEADS * HEAD_DIM
N_FOURIER_GROUPS = 8
FOURIER_GROUP_DIM = 64
FOURIER_WIDTH = N_FOURIER_GROUPS * FOURIER_GROUP_DIM
GATE_WIDTH = 2 * D_MODEL
IN_WIDTH = FOURIER_WIDTH + ATTN_WIDTH + 2 * KV_WIDTH + GATE_WIDTH
PEER_HEADS = 8
PEER_NKEYS = 128
PEER_N_EXPERTS = PEER_NKEYS * PEER_NKEYS
PEER_TOPK = 16
PEER_KEY_DIM = 128
PLE_DIM = 256
DEPTH = 2
DEEPNORM_ALPHA = (2 * DEPTH) ** 0.25
LN_EPS = 1e-5
NEG_INF = -1e30

LANES = 128
SUBLANES = 8
BF16_ROWS = 16
VMEM_LIMIT = 52 * 1024 * 1024

_O_F = 0
_O_Q = _O_F + FOURIER_WIDTH
_O_K = _O_Q + ATTN_WIDTH
_O_V = _O_K + KV_WIDTH
_O_G = _O_V + KV_WIDTH

_CAND_GROUPS = [(0, 0), (0, 8), (1, 0), (2, 0), (3, 0), (4, 0), (5, 0), (6, 0), (7, 0)]
_CAND_ROWS = 8 * (len(_CAND_GROUPS) + 1)
_BIG_INDEX = 1.0e6


def _params(semantics, flags=None):
    return pltpu.CompilerParams(dimension_semantics=semantics, vmem_limit_bytes=VMEM_LIMIT, flags=flags)


def _layer_norm(x, g, b):
    mu = jnp.mean(x, axis=-1, keepdims=True)
    xc = x - mu
    var = jnp.mean(xc * xc, axis=-1, keepdims=True)
    return xc * lax.rsqrt(var + LN_EPS) * g + b


def _dot(a, b):
    return jnp.dot(a, b, preferred_element_type=F32)


def _dot_nt(a, b):
    return lax.dot_general(a, b, (((1,), (1,)), ((), ())), preferred_element_type=F32)


def _inproj_kernel(*refs, apply_ln):
    if apply_ln:
        x_ref, g_ref, b_ref, w_ref, mch_ref, xn_ref, ab_ref, q_ref, k_ref, v_ref, gt_ref = refs
        x = _layer_norm(x_ref[...], g_ref[...], b_ref[...])
        xn_ref[...] = x
    else:
        x_ref, w_ref, mch_ref, ab_ref, q_ref, k_ref, v_ref, gt_ref = refs
        x = x_ref[...]
    xb = x.astype(BF16)
    f = _dot(xb, w_ref[:, _O_F:_O_Q])
    ab_ref[...] = _dot(f.astype(BF16), mch_ref[...]).astype(BF16)
    q_ref[...] = _dot(xb, w_ref[:, _O_Q:_O_K]).astype(BF16)
    k_ref[...] = _dot(xb, w_ref[:, _O_K:_O_V]).astype(BF16)
    v_ref[...] = _dot(xb, w_ref[:, _O_V:_O_G]).astype(BF16)
    gt_ref[...] = jax.nn.sigmoid(_dot(xb, w_ref[:, _O_G:IN_WIDTH])).astype(BF16)


def _inproj(x, ln, w_in, mch, tile):
    t = x.shape[0]
    apply_ln = ln is not None
    row = lambda width: pl.BlockSpec((tile, width), lambda i: (i, 0))
    full = lambda a: pl.BlockSpec(a.shape, lambda i: (0,) * a.ndim)
    ins = [x] + ([ln[0], ln[1]] if apply_ln else []) + [w_in, mch]
    in_specs = [row(D_MODEL)] + ([full(ln[0]), full(ln[1])] if apply_ln else []) + [full(w_in), full(mch)]
    widths = [2 * FOURIER_WIDTH, ATTN_WIDTH, KV_WIDTH, KV_WIDTH, GATE_WIDTH]
    out_shape = [jax.ShapeDtypeStruct((t, w), BF16) for w in widths]
    out_specs = [row(w) for w in widths]
    if apply_ln:
        out_shape = [jax.ShapeDtypeStruct((t, D_MODEL), F32)] + out_shape
        out_specs = [row(D_MODEL)] + out_specs
    return pl.pallas_call(
        functools.partial(_inproj_kernel, apply_ln=apply_ln),
        grid=(t // tile,),
        in_specs=in_specs,
        out_specs=out_specs,
        out_shape=out_shape,
        compiler_params=_params(("parallel",)),
        name="inproj_ln" if apply_ln else "inproj",
    )(*ins)


def _seqdft_kernel(c_ref, s_ref, a_ref, b_ref, y_ref):
    y_ref[...] = (_dot(c_ref[...], a_ref[...]) + _dot(s_ref[...], b_ref[...])).astype(BF16)


def _seqdft(ab, cmat, smat, tile):
    bsz, s, _ = ab.shape
    return pl.pallas_call(
        _seqdft_kernel,
        grid=(bsz, s // tile),
        in_specs=[
            pl.BlockSpec((tile, s), lambda b, m: (m, 0)),
            pl.BlockSpec((tile, s), lambda b, m: (m, 0)),
            pl.BlockSpec((None, s, FOURIER_WIDTH), lambda b, m: (b, 0, 0)),
            pl.BlockSpec((None, s, FOURIER_WIDTH), lambda b, m: (b, 0, 1)),
        ],
        out_specs=pl.BlockSpec((None, tile, FOURIER_WIDTH), lambda b, m: (b, m, 0)),
        out_shape=jax.ShapeDtypeStruct((bsz, s, FOURIER_WIDTH), BF16),
        compiler_params=_params(("parallel", "parallel")),
        name="seqdft",
    )(cmat, smat, ab, ab)


def _attn_kernel(sink_ref, q_ref, kp_ref, kc_ref, kn_ref, vp_ref, vc_ref, vn_ref, bias_ref, o_ref, *, nblocks):
    n = pl.program_id(1)
    col = lax.broadcasted_iota(jnp.int32, (1, 3 * ATTN_BLOCK), 1)
    off_edge = ((col < ATTN_BLOCK) & (n == 0)) | ((col >= 2 * ATTN_BLOCK) & (n == nblocks - 1))
    edge = jnp.where(off_edge, NEG_INF, 0.0).astype(F32)
    kcat = jnp.concatenate([kp_ref[...], kc_ref[...], kn_ref[...]], axis=0)
    vcat = jnp.concatenate([vp_ref[...], vc_ref[...], vn_ref[...]], axis=0)
    vones = jnp.concatenate([vcat, jnp.ones_like(vcat)], axis=1)
    lane = lax.broadcasted_iota(jnp.int32, (ATTN_BLOCK, KV_WIDTH), 1)
    low = lane < HEAD_DIM
    for g in range(Q_GROUP):
        qg = q_ref[:, g * KV_WIDTH:(g + 1) * KV_WIDTH]
        halves = []
        for kh in range(N_KV_HEADS):
            h = kh * Q_GROUP + g
            qm = jnp.where(low if kh == 0 else jnp.logical_not(low), qg, jnp.zeros_like(qg))
            s = _dot_nt(qm, kcat) + bias_ref[h] + edge
            sink = sink_ref[h]
            m = jnp.maximum(jnp.max(s, axis=-1, keepdims=True), sink)
            p = jnp.exp(s - m)
            pv = _dot(p.astype(BF16), vones)
            halves.append(pv[:, :KV_WIDTH] / (pv[:, KV_WIDTH:] + jnp.exp(sink - m)))
        o_ref[:, g * KV_WIDTH:(g + 1) * KV_WIDTH] = jnp.where(low, halves[0], halves[1]).astype(BF16)


def _attention(q, k, v, bias, sink):
    bsz, s, _ = q.shape
    nb = s // ATTN_BLOCK
    kv_spec = lambda fn: pl.BlockSpec((None, ATTN_BLOCK, KV_WIDTH), fn)
    prev = lambda b, n: (b, jnp.maximum(n - 1, 0), 0)
    cur = lambda b, n: (b, n, 0)
    nxt = lambda b, n: (b, jnp.minimum(n + 1, nb - 1), 0)
    return pl.pallas_call(
        functools.partial(_attn_kernel, nblocks=nb),
        grid=(bsz, nb),
        in_specs=[
            pl.BlockSpec(memory_space=pltpu.SMEM),
            pl.BlockSpec((None, ATTN_BLOCK, ATTN_WIDTH), cur),
            kv_spec(prev), kv_spec(cur), kv_spec(nxt),
            kv_spec(prev), kv_spec(cur), kv_spec(nxt),
            pl.BlockSpec(bias.shape, lambda b, n: (0, 0, 0)),
        ],
        out_specs=pl.BlockSpec((None, ATTN_BLOCK, ATTN_WIDTH), cur),
        out_shape=jax.ShapeDtypeStruct((bsz, s, ATTN_WIDTH), BF16),
        compiler_params=_params(("parallel", "parallel")),
        name="window_attn",
    )(sink, q, k, k, k, v, v, v, bias)


def _merge_kernel(x_ref, y_ref, a_ref, gt_ref, wfo_ref, wao_ref, wout_ref, g_ref, b_ref, o_ref):
    f = _dot(y_ref[...], wfo_ref[...])
    a = _dot(a_ref[...], wao_ref[...])
    merged = gt_ref[:, :D_MODEL].astype(F32) * f + gt_ref[:, D_MODEL:].astype(F32) * a
    o = _dot(merged.astype(BF16), wout_ref[...])
    o_ref[...] = _layer_norm(DEEPNORM_ALPHA * x_ref[...] + o, g_ref[...], b_ref[...])


def _merge(x, y, a, gt, wfo, wao, wout, g, b, tile):
    t = x.shape[0]
    row = lambda width: pl.BlockSpec((tile, width), lambda i: (i, 0))
    full = lambda arr: pl.BlockSpec(arr.shape, lambda i: (0,) * arr.ndim)
    return pl.pallas_call(
        _merge_kernel,
        grid=(t // tile,),
        in_specs=[row(D_MODEL), row(FOURIER_WIDTH), row(ATTN_WIDTH), row(GATE_WIDTH),
                  full(wfo), full(wao), full(wout), full(g), full(b)],
        out_specs=row(D_MODEL),
        out_shape=jax.ShapeDtypeStruct((t, D_MODEL), F32),
        compiler_params=_params(("parallel",)),
        name="merge_ln1",
    )(x, y, a, gt, wfo, wao, wout, g, b)


def _extract_sorted(s, ids, count, vals_ref):
    def body(r, carry):
        s, rank = carry
        m = jnp.max(s, axis=0, keepdims=True)
        first = jnp.min(jnp.where(s == m, ids, _BIG_INDEX), axis=0, keepdims=True)
        sel = ids == first
        if vals_ref is not None:
            vals_ref[pl.ds(r, 1), :] = m
        rank = jnp.where(sel, lax.convert_element_type(r, F32), rank)
        s = jnp.where(sel, -jnp.inf, s)
        return s, rank

    rank0 = jnp.full(s.shape, float(count), F32)
    s, rank = lax.fori_loop(0, count, body, (s, rank0))
    return rank, s


def _select_exact(s1, s2, key_ids, cand_ids, cand_mask, v1_scr, v2_scr):
    rank1, _ = _extract_sorted(s1, key_ids, PEER_TOPK, v1_scr)
    rank2, _ = _extract_sorted(s2, key_ids, PEER_TOPK, v2_scr)
    v1 = v1_scr[...]
    v2 = v2_scr[...]
    groups = [v1[r1:r1 + 1] + v2[lo:lo + SUBLANES] for r1, lo in _CAND_GROUPS]
    groups.append(v1[SUBLANES:] + v2[0:1])
    cand = jnp.concatenate(groups, axis=0) + cand_mask
    crank, _ = _extract_sorted(cand, cand_ids, PEER_TOPK, None)
    chosen = crank < float(PEER_TOPK)
    pexp = jnp.where(chosen, jnp.exp(cand - cand[0:1]), 0.0)
    inv_z = 1.0 / jnp.sum(pexp, axis=0, keepdims=True)
    cnt = chosen.astype(F32)
    counts = [jnp.sum(cnt[0:2 * SUBLANES], axis=0, keepdims=True)]
    for gi in range(2, len(_CAND_GROUPS)):
        counts.append(jnp.sum(cnt[gi * SUBLANES:(gi + 1) * SUBLANES], axis=0, keepdims=True))
    tail = cnt[len(_CAND_GROUPS) * SUBLANES:]
    c1 = jnp.zeros_like(s1)
    for r in range(PEER_TOPK):
        cr = counts[r] if r < SUBLANES else tail[r - SUBLANES:r - SUBLANES + 1]
        c1 = jnp.where(rank1 == float(r), cr, c1)
    e1 = jnp.exp(s1 - v1[0:1]) * inv_z
    e2 = jnp.exp(s2 - v2[0:1])
    return e1, c1, e2, rank2


def _merge_exchange_pairs(n):
    pairs = []
    t = max(1, math.ceil(math.log2(n)))
    p = 1 << (t - 1)
    while p > 0:
        q, r, d = 1 << (t - 1), 0, p
        while d > 0:
            pairs.extend((i, i + d) for i in range(n - d) if (i & p) == r)
            d, q, r = q - p, q >> 1, p
        p >>= 1
    return pairs


def _compare_exchange(v, i, j):
    v[i], v[j] = jnp.maximum(v[i], v[j]), jnp.minimum(v[i], v[j])


def _sort_desc(v):
    v = list(v)
    for i, j in _merge_exchange_pairs(len(v)):
        _compare_exchange(v, i, j)
    return v


def _bitonic_merge(v):
    v = list(v)
    d = len(v) // 2
    while d:
        for k in range(len(v)):
            if not k & d:
                _compare_exchange(v, k, k + d)
        d //= 2
    return v


def _across_sublanes(x, op):
    for shift in (4, 2, 1):
        x = op(x, pltpu.roll(x, shift, 0))
    return x


def _top16_of_sublane_lists(lists):
    for shift in (4, 2, 1):
        other = [pltpu.roll(x, shift, 0) for x in lists]
        n = len(lists)
        merged = []
        for k in range(PEER_TOPK):
            mine = lists[k] if k < n else None
            theirs = other[PEER_TOPK - 1 - k] if PEER_TOPK - 1 - k < n else None
            merged.append(mine if theirs is None else theirs if mine is None else jnp.maximum(mine, theirs))
        lists = _bitonic_merge(merged)
    return lists


def _select_sorted(s1, s2, cand_mask):
    nv = PEER_NKEYS // SUBLANES
    a1 = [s1[j * SUBLANES:(j + 1) * SUBLANES] for j in range(nv)]
    a2 = [s2[j * SUBLANES:(j + 1) * SUBLANES] for j in range(nv)]
    v1 = _top16_of_sublane_lists(_sort_desc(a1))
    v2 = _top16_of_sublane_lists(_sort_desc(a2))
    sub = lax.broadcasted_iota(jnp.int32, (SUBLANES, LANES), 0)

    def by_sublane(vals):
        out = vals[SUBLANES - 1]
        for j in range(SUBLANES - 2, -1, -1):
            out = jnp.where(sub == j, vals[j], out)
        return out

    v2_nat = {0: by_sublane(v2[:SUBLANES]), SUBLANES: by_sublane(v2[SUBLANES:])}
    groups = [v1[r1] + v2_nat[lo] for r1, lo in _CAND_GROUPS]
    groups.append(by_sublane(v1[SUBLANES:]) + v2[0])
    groups = [g + cand_mask[i * SUBLANES:(i + 1) * SUBLANES] for i, g in enumerate(groups)]
    top = _top16_of_sublane_lists(_sort_desc(groups))
    thr = top[PEER_TOPK - 1]
    z = None
    for k in range(PEER_TOPK):
        term = jnp.exp(top[k] - top[0])
        z = term if z is None else z + term
    inv_z = 1.0 / z
    picked = [jnp.where(g >= thr, 1.0, 0.0) for g in groups[:-1]]
    add = lambda a, b: a + b
    counts = [_across_sublanes(picked[0] + picked[1], add)]
    counts += [_across_sublanes(picked[r + 1], add) for r in range(1, SUBLANES)]
    counts += [jnp.where(v1[r] + v2[0] >= thr, 1.0, 0.0) for r in range(SUBLANES, PEER_TOPK)]
    e1, c1, e2, r2 = [], [], [], []
    c_total = None
    r_total = None
    for j in range(nv):
        c = jnp.zeros((SUBLANES, LANES), F32)
        r = jnp.full((SUBLANES, LANES), float(PEER_TOPK), F32)
        for k in range(PEER_TOPK - 1, -1, -1):
            c = jnp.where(a1[j] >= v1[k], counts[k], c)
            r = jnp.where(a2[j] >= v2[k], float(k), r)
        c1.append(c)
        r2.append(r)
        e1.append(jnp.exp(a1[j] - v1[0]) * inv_z)
        e2.append(jnp.exp(a2[j] - v2[0]))
        c_total = c if c_total is None else c_total + c
        r_total = (float(PEER_TOPK) - r) if r_total is None else r_total + (float(PEER_TOPK) - r)
    c_total = _across_sublanes(c_total, add)
    r_total = _across_sublanes(r_total, add)
    distinct_total = float(PEER_TOPK * (PEER_TOPK + 1) // 2)
    bad = jnp.where((c_total != float(PEER_TOPK)) | (r_total != distinct_total), 1.0, 0.0)
    cat = lambda parts: jnp.concatenate(parts, axis=0)
    return cat(e1), cat(c1), cat(e2), cat(r2), bad


def _select_kernel(x_ref, pq_ref, keys_ref, meta_ref, xt_ref, e1_ref, c1_ref, e2_ref, r2_ref,
                   q_scr, v1_scr, v2_scr, *, tile):
    x = x_ref[...]
    xt_ref[...] = x.T.astype(BF16)
    q = _dot(x.astype(BF16), pq_ref[...])
    for hp in range(2 * PEER_HEADS):
        q_scr[hp] = q[:, hp * PEER_KEY_DIM:(hp + 1) * PEER_KEY_DIM].astype(BF16)

    chunks = [slice(c * LANES, (c + 1) * LANES) for c in range(tile // LANES)]

    def scores(h):
        return tuple(_dot_nt(keys_ref[h, half], q_scr[2 * h + half, tok, :]) for tok in chunks for half in (0, 1))

    def head_body(h, s_all):
        s_next = scores(jnp.minimum(h + 1, PEER_HEADS - 1))

        def emit(tok, e1, c1, e2, r2):
            e1_ref[h, :, tok] = e1
            c1_ref[h, :, tok] = c1
            e2_ref[h, :, tok] = e2.astype(BF16)
            r2_ref[h, :, tok] = r2.astype(BF16)

        flags = []
        for c, tok in enumerate(chunks):
            e1, c1, e2, r2, bad = _select_sorted(s_all[2 * c], s_all[2 * c + 1], meta_ref[1])
            emit(tok, e1, c1, e2, r2)
            flags.append(jnp.max(bad))
        for c, tok in enumerate(chunks):
            @pl.when(flags[c] > 0.0)
            def _():
                key_ids = lax.broadcasted_iota(jnp.int32, (PEER_NKEYS, LANES), 0).astype(F32)
                emit(tok, *_select_exact(s_all[2 * c], s_all[2 * c + 1], key_ids, meta_ref[0], meta_ref[1],
                                         v1_scr, v2_scr))
        return s_next

    lax.fori_loop(0, PEER_HEADS, head_body, scores(0))


def _cand_meta():
    ids = np.full((_CAND_ROWS, LANES), _BIG_INDEX, np.float32)
    mask = np.full((_CAND_ROWS, LANES), -np.inf, np.float32)
    for gi, (r1, lo) in enumerate(_CAND_GROUPS):
        for j in range(SUBLANES):
            r2 = lo + j
            if (r1 + 1) * (r2 + 1) <= PEER_TOPK:
                ids[gi * SUBLANES + j] = r1 * PEER_TOPK + r2
                mask[gi * SUBLANES + j] = 0.0
    base = len(_CAND_GROUPS) * SUBLANES
    for j in range(SUBLANES):
        ids[base + j] = (SUBLANES + j) * PEER_TOPK
        mask[base + j] = 0.0
    return np.stack([ids, mask])


def _peer_select(x, pq, keys, tile):
    t = x.shape[0]
    meta = jnp.asarray(_cand_meta())
    dense = lambda dtype: jax.ShapeDtypeStruct((PEER_HEADS, PEER_NKEYS, t), dtype)
    dense_spec = pl.BlockSpec((PEER_HEADS, PEER_NKEYS, tile), lambda i: (0, 0, i))
    full = lambda arr: pl.BlockSpec(arr.shape, lambda i: (0,) * arr.ndim)
    return pl.pallas_call(
        functools.partial(_select_kernel, tile=tile),
        grid=(t // tile,),
        in_specs=[pl.BlockSpec((tile, D_MODEL), lambda i: (i, 0)), full(pq), full(keys), full(meta)],
        out_specs=[pl.BlockSpec((D_MODEL, tile), lambda i: (0, i))] + [dense_spec] * 4,
        out_shape=[jax.ShapeDtypeStruct((D_MODEL, t), BF16), dense(F32), dense(F32), dense(BF16), dense(BF16)],
        scratch_shapes=[
            pltpu.VMEM((2 * PEER_HEADS, tile, PEER_KEY_DIM), BF16),
            pltpu.VMEM((PEER_TOPK, LANES), F32),
            pltpu.VMEM((PEER_TOPK, LANES), F32),
        ],
        compiler_params=_params(("parallel",)),
        name="peer_select",
    )(x, pq, keys, meta)


def _activations(xt_ref, u_ref, act_ref):
    z = _dot(u_ref[...], xt_ref[...])
    half = z * (1.0 / math.sqrt(2.0))
    act_ref[...] = (half + half * lax.erf(z)).astype(BF16)


def _gate_activations(block, e1_ref, c1_ref, e2_ref, r2_ref, act_ref, w_ref, rows_per_block):
    tile = act_ref.shape[1]
    for jj in range(rows_per_block):
        j = block * rows_per_block + jj
        c1 = [jnp.broadcast_to(c1_ref[h, pl.ds(j, 1), :], (BF16_ROWS, tile)).astype(BF16)
              for h in range(PEER_HEADS)]
        e1 = [jnp.broadcast_to(e1_ref[h, pl.ds(j, 1), :], (BF16_ROWS, tile)).astype(BF16)
              for h in range(PEER_HEADS)]
        for g in range(PEER_NKEYS // BF16_ROWS):
            keys = slice(g * BF16_ROWS, (g + 1) * BF16_ROWS)
            gate = None
            for h in range(PEER_HEADS):
                picked = jnp.where(r2_ref[h, keys, :] < c1[h], e2_ref[h, keys, :], jnp.zeros((), BF16))
                term = picked * e1[h]
                gate = term if gate is None else gate + term
            rows = slice(jj * PEER_NKEYS + g * BF16_ROWS, jj * PEER_NKEYS + (g + 1) * BF16_ROWS)
            w_ref[rows, :] = gate * act_ref[rows, :]


def _dense_kernel(xt_ref, e1_ref, c1_ref, e2_ref, r2_ref, u_ref, vtp_ref, *refs, rows_per_block, nb):
    vt_refs, o_ref = refs[:nb], refs[nb]
    act_scr, w_scr = refs[nb + 1:2 * nb + 1], refs[2 * nb + 1:]
    e = pl.program_id(1)
    nrows = rows_per_block * PEER_NKEYS

    @pl.when(e == 0)
    def _():
        o_ref[...] = jnp.zeros_like(o_ref)
        w_scr[nb - 1][...] = jnp.zeros_like(w_scr[nb - 1])

    def activations(i):
        _activations(xt_ref, u_ref.at[i * nrows:(i + 1) * nrows, :], act_scr[i])

    activations(0)
    o_ref[...] += _dot(vtp_ref[...], w_scr[nb - 1][...])
    for i in range(nb):
        _gate_activations(nb * e + i, e1_ref, c1_ref, e2_ref, r2_ref, act_scr[i], w_scr[i], rows_per_block)
        if i + 1 < nb:
            activations(i + 1)
            o_ref[...] += _dot(vt_refs[i][...], w_scr[i][...])

    @pl.when(e == pl.num_programs(1) - 1)
    def _():
        o_ref[...] += _dot(vt_refs[nb - 1][...], w_scr[nb - 1][...])


def _peer_dense(xt, e1, c1, e2, r2, u, vt, tile, eblock, nb):
    t = xt.shape[1]
    row_spec = pl.BlockSpec((PEER_HEADS, PEER_NKEYS, tile), lambda i, e: (0, 0, i))
    vt_spec = lambda fn: pl.BlockSpec((D_MODEL, eblock), fn)
    return pl.pallas_call(
        functools.partial(_dense_kernel, rows_per_block=eblock // PEER_NKEYS, nb=nb),
        grid=(t // tile, PEER_N_EXPERTS // (nb * eblock)),
        in_specs=[pl.BlockSpec((D_MODEL, tile), lambda i, e: (0, i))] + [row_spec] * 4 + [
            pl.BlockSpec((nb * eblock, D_MODEL), lambda i, e: (e, 0)),
            vt_spec(lambda i, e: (0, jnp.maximum(nb * e - 1, 0))),
        ] + [vt_spec(functools.partial(lambda i, e, k: (0, nb * e + k), k=k)) for k in range(nb)],
        out_specs=pl.BlockSpec((D_MODEL, tile), lambda i, e: (0, i)),
        out_shape=jax.ShapeDtypeStruct((D_MODEL, t), F32),
        scratch_shapes=[pltpu.VMEM((eblock, tile), BF16)] * (2 * nb),
        compiler_params=_params(("parallel", "arbitrary")),
        name="peer_dense",
    )(xt, e1, c1, e2, r2, u, vt, *([vt] * nb))


def _final_kernel(x_ref, cmt_ref, p_ref, wg_ref, wp_ref, g_ref, b_ref, o_ref):
    x = x_ref[...]
    gate = jax.nn.sigmoid(_dot(x.astype(BF16), wg_ref[...]))
    ple = gate * _dot(p_ref[...].astype(BF16), wp_ref[...])
    o_ref[...] = _layer_norm(DEEPNORM_ALPHA * x + cmt_ref[...].T + ple, g_ref[...], b_ref[...])


def _final(x, cmt, p, wg, wp, g, b, tile):
    t = x.shape[0]
    full = lambda arr: pl.BlockSpec(arr.shape, lambda i: (0,) * arr.ndim)
    return pl.pallas_call(
        _final_kernel,
        grid=(t // tile,),
        in_specs=[pl.BlockSpec((tile, D_MODEL), lambda i: (i, 0)),
                  pl.BlockSpec((D_MODEL, tile), lambda i: (0, i)),
                  pl.BlockSpec((tile, PLE_DIM), lambda i: (i, 0)),
                  full(wg), full(wp), full(g), full(b)],
        out_specs=pl.BlockSpec((tile, D_MODEL), lambda i: (i, 0)),
        out_shape=jax.ShapeDtypeStruct((t, D_MODEL), F32),
        compiler_params=_params(("parallel",)),
        name="ple_ln2",
    )(x, cmt, p, wg, wp, g, b)


def _channel_dft():
    c = np.arange(FOURIER_GROUP_DIM)
    ang = 2.0 * np.pi * ((c[:, None] * c[None, :]) % FOURIER_GROUP_DIM) / FOURIER_GROUP_DIM
    eye = np.eye(N_FOURIER_GROUPS)
    return np.concatenate([np.kron(eye, np.cos(ang)), np.kron(eye, -np.sin(ang))], axis=1)


def _sequence_dft(s, scale):
    s_lo = 64
    s_hi = s // s_lo
    k = np.arange(s)
    ang_hi = 2.0 * np.pi * ((k[:, None] * np.arange(s_hi)[None, :]) % s_hi) / s_hi
    ang_lo = 2.0 * np.pi * ((k[:, None] * np.arange(s_lo)[None, :]) % s) / s
    ch = jnp.asarray(np.cos(ang_hi) * scale, F32)[:, :, None]
    sh = jnp.asarray(np.sin(ang_hi) * scale, F32)[:, :, None]
    cl = jnp.asarray(np.cos(ang_lo), F32)[:, None, :]
    sl = jnp.asarray(np.sin(ang_lo), F32)[:, None, :]
    cmat = (ch * cl - sh * sl).reshape(s, s).astype(BF16)
    smat = (sh * cl + ch * sl).reshape(s, s).astype(BF16)
    return cmat, smat


def _attn_bias():
    qi = np.arange(ATTN_BLOCK)
    kj = np.arange(3 * ATTN_BLOCK)
    rel = np.abs(qi[:, None] + ATTN_BLOCK - kj[None, :]).astype(np.float64)
    slopes = np.exp2(-8.0 * np.arange(1, N_HEADS + 1) / N_HEADS)
    bias = np.where(rel[None] <= WINDOW, -slopes[:, None, None] * rel[None], NEG_INF)
    return bias.astype(np.float32)


def _pick_tile(n, pref):
    tile = min(n, pref)
    assert n % tile == 0, (n, tile)
    return tile


def _trunk(x, p_all, consts, emb_ln, layers):
    bsz, s, _ = x.shape
    t = bsz * s
    mch, cmat, smat, bias = consts
    x = x.reshape(t, D_MODEL)
    row_tile = _pick_tile(t, 512)
    for i, lw in enumerate(layers):
        if i == 0:
            x, ab, q, k, v, gt = _inproj(x, emb_ln, lw["w_in"], mch, row_tile)
        else:
            ab, q, k, v, gt = _inproj(x, None, lw["w_in"], mch, row_tile)
        y = _seqdft(ab.reshape(bsz, s, 2 * FOURIER_WIDTH), cmat, smat, _pick_tile(s, 256))
        a = _attention(q.reshape(bsz, s, ATTN_WIDTH), k.reshape(bsz, s, KV_WIDTH),
                       v.reshape(bsz, s, KV_WIDTH), bias, lw["sink"])
        x = _merge(x, y.reshape(t, FOURIER_WIDTH), a.reshape(t, ATTN_WIDTH), gt,
                   lw["w_fo"], lw["w_ao"], lw["w_out"], lw["ln1_g"], lw["ln1_b"], row_tile)
        xt, e1, c1, e2, r2 = _peer_select(x, lw["pq"], lw["keys"], _pick_tile(t, 256))
        cmt = _peer_dense(xt, e1, c1, e2, r2, lw["u"], lw["vt"], _pick_tile(t, 512), 512, 2)
        x = _final(x, cmt, p_all[i].reshape(t, PLE_DIM), lw["wg"], lw["wp"], lw["ln2_g"], lw["ln2_b"],
                   row_tile)
    return x.reshape(bsz, s, D_MODEL)


def _prepare_layer(i, w_in, attn_sink, w_fourier_out, w_attn_out, w_out, ln1_g, ln1_b, peer_w_q,
                   peer_keys, peer_u, peer_v, ple_w_gate, ple_w_proj, ln2_g, ln2_b):
    w = w_in[i]
    wq = w[:, _O_Q:_O_K].reshape(D_MODEL, N_KV_HEADS, Q_GROUP, HEAD_DIM).transpose(0, 2, 1, 3)
    wq = wq.reshape(D_MODEL, ATTN_WIDTH) * (HEAD_DIM ** -0.5)
    w_perm = jnp.concatenate([w[:, :_O_Q], wq, w[:, _O_K:]], axis=1).astype(BF16)
    wao = w_attn_out[i].reshape(N_KV_HEADS, Q_GROUP, HEAD_DIM, D_MODEL).transpose(1, 0, 2, 3)
    row = lambda a: a[i].reshape(1, D_MODEL).astype(F32)
    return dict(
        w_in=w_perm, sink=attn_sink[i].astype(F32),
        w_fo=w_fourier_out[i].astype(BF16), w_ao=wao.reshape(ATTN_WIDTH, D_MODEL).astype(BF16),
        w_out=w_out[i].astype(BF16), ln1_g=row(ln1_g), ln1_b=row(ln1_b),
        pq=peer_w_q[i].astype(BF16), keys=peer_keys[i].astype(BF16),
        u=(peer_u[i] * (1.0 / math.sqrt(2.0))).astype(BF16), vt=peer_v[i].astype(BF16).T,
        wg=ple_w_gate[i].astype(BF16), wp=ple_w_proj[i].astype(BF16), ln2_g=row(ln2_g), ln2_b=row(ln2_b),
    )


def kernel(x_prompt, x_sample, p_prompt, p_sample, emb_ln_g, emb_ln_b, w_in, attn_sink, w_fourier_out, w_attn_out, w_out, ln1_g, ln1_b, peer_w_q, peer_keys, peer_u, peer_v, ple_w_gate, ple_w_proj, ln2_g, ln2_b):
    depth = w_in.shape[0]
    layers = [_prepare_layer(i, w_in, attn_sink, w_fourier_out, w_attn_out, w_out, ln1_g, ln1_b, peer_w_q,
                             peer_keys, peer_u, peer_v, ple_w_gate, ple_w_proj, ln2_g, ln2_b)
              for i in range(depth)]
    emb_ln = (emb_ln_g.reshape(1, D_MODEL).astype(F32), emb_ln_b.reshape(1, D_MODEL).astype(F32))
    mch = jnp.asarray(_channel_dft() / math.sqrt(FOURIER_GROUP_DIM), BF16)
    bias = jnp.asarray(_attn_bias())
    outs = []
    for x, p in ((x_prompt, p_prompt), (x_sample, p_sample)):
        s = x.shape[1]
        cmat, smat = _sequence_dft(s, 1.0 / math.sqrt(s))
        outs.append(_trunk(x, p, (mch, cmat, smat, bias), emb_ln, layers))
    return tuple(outs)
```

```python
import functools
import math

import numpy as np
import jax
import jax.numpy as jnp
from jax import lax
from jax.experimental import pallas as pl
from jax.experimental.pallas import tpu as pltpu

F32 = jnp.float32
BF16 = jnp.bfloat16

D_MODEL = 1024
N_HEADS = 8
N_KV_HEADS = 2
Q_GROUP = N_HEADS // N_KV_HEADS
HEAD_DIM = 64
WINDOW = 128
ATTN_BLOCK = 128
ATTN_WIDTH = N_HEADS * HEAD_DIM
KV_WIDTH = N_KV_HEADS * HEAD_DIM
N_FOURIER_GROUPS = 8
FOURIER_GROUP_DIM = 64
FOURIER_WIDTH = N_FOURIER_GROUPS * FOURIER_GROUP_DIM
GATE_WIDTH = 2 * D_MODEL
IN_WIDTH = FOURIER_WIDTH + ATTN_WIDTH + 2 * KV_WIDTH + GATE_WIDTH
PEER_HEADS = 8
PEER_NKEYS = 128
PEER_N_EXPERTS = PEER_NKEYS * PEER_NKEYS
PEER_TOPK = 16
PEER_KEY_DIM = 128
PLE_DIM = 256
DEPTH = 2
DEEPNORM_ALPHA = (2 * DEPTH) ** 0.25
LN_EPS = 1e-5
NEG_INF = -1e30

LANES = 128
SUBLANES = 8
BF16_ROWS = 16
VMEM_LIMIT = 52 * 1024 * 1024

_O_F = 0
_O_Q = _O_F + FOURIER_WIDTH
_O_K = _O_Q + ATTN_WIDTH
_O_V = _O_K + KV_WIDTH
_O_G = _O_V + KV_WIDTH

_CAND_GROUPS = [(0, 0), (0, 8), (1, 0), (2, 0), (3, 0), (4, 0), (5, 0), (6, 0), (7, 0)]
_CAND_ROWS = 8 * (len(_CAND_GROUPS) + 1)
_BIG_INDEX = 1.0e6


def _params(semantics, flags=None):
    return pltpu.CompilerParams(dimension_semantics=semantics, vmem_limit_bytes=VMEM_LIMIT, flags=flags)


def _layer_norm(x, g, b):
    mu = jnp.mean(x, axis=-1, keepdims=True)
    xc = x - mu
    var = jnp.mean(xc * xc, axis=-1, keepdims=True)
    return xc * lax.rsqrt(var + LN_EPS) * g + b


def _dot(a, b):
    return jnp.dot(a, b, preferred_element_type=F32)


def _dot_nt(a, b):
    return lax.dot_general(a, b, (((1,), (1,)), ((), ())), preferred_element_type=F32)


def _inproj_kernel(*refs, apply_ln):
    if apply_ln:
        x_ref, g_ref, b_ref, w_ref, mch_ref, xn_ref, ab_ref, q_ref, k_ref, v_ref, gt_ref = refs
        x = _layer_norm(x_ref[...], g_ref[...], b_ref[...])
        xn_ref[...] = x
    else:
        x_ref, w_ref, mch_ref, ab_ref, q_ref, k_ref, v_ref, gt_ref = refs
        x = x_ref[...]
    xb = x.astype(BF16)
    f = _dot(xb, w_ref[:, _O_F:_O_Q])
    ab_ref[...] = _dot(f.astype(BF16), mch_ref[...]).astype(BF16)
    q_ref[...] = _dot(xb, w_ref[:, _O_Q:_O_K]).astype(BF16)
    k_ref[...] = _dot(xb, w_ref[:, _O_K:_O_V]).astype(BF16)
    v_ref[...] = _dot(xb, w_ref[:, _O_V:_O_G]).astype(BF16)
    gt_ref[...] = jax.nn.sigmoid(_dot(xb, w_ref[:, _O_G:IN_WIDTH])).astype(BF16)


def _inproj(x, ln, w_in, mch, tile):
    t = x.shape[0]
    apply_ln = ln is not None
    row = lambda width: pl.BlockSpec((tile, width), lambda i: (i, 0))
    full = lambda a: pl.BlockSpec(a.shape, lambda i: (0,) * a.ndim)
    ins = [x] + ([ln[0], ln[1]] if apply_ln else []) + [w_in, mch]
    in_specs = [row(D_MODEL)] + ([full(ln[0]), full(ln[1])] if apply_ln else []) + [full(w_in), full(mch)]
    widths = [2 * FOURIER_WIDTH, ATTN_WIDTH, KV_WIDTH, KV_WIDTH, GATE_WIDTH]
    out_shape = [jax.ShapeDtypeStruct((t, w), BF16) for w in widths]
    out_specs = [row(w) for w in widths]
    if apply_ln:
        out_shape = [jax.ShapeDtypeStruct((t, D_MODEL), F32)] + out_shape
        out_specs = [row(D_MODEL)] + out_specs
    return pl.pallas_call(
        functools.partial(_inproj_kernel, apply_ln=apply_ln),
        grid=(t // tile,),
        in_specs=in_specs,
        out_specs=out_specs,
        out_shape=out_shape,
        compiler_params=_params(("parallel",)),
        name="inproj_ln" if apply_ln else "inproj",
    )(*ins)


def _seqdft_kernel(c_ref, s_ref, a_ref, b_ref, y_ref):
    y_ref[...] = (_dot(c_ref[...], a_ref[...]) + _dot(s_ref[...], b_ref[...])).astype(BF16)


def _seqdft(ab, cmat, smat, tile):
    bsz, s, _ = ab.shape
    return pl.pallas_call(
        _seqdft_kernel,
        grid=(bsz, s // tile),
        in_specs=[
            pl.BlockSpec((tile, s), lambda b, m: (m, 0)),
            pl.BlockSpec((tile, s), lambda b, m: (m, 0)),
            pl.BlockSpec((None, s, FOURIER_WIDTH), lambda b, m: (b, 0, 0)),
            pl.BlockSpec((None, s, FOURIER_WIDTH), lambda b, m: (b, 0, 1)),
        ],
        out_specs=pl.BlockSpec((None, tile, FOURIER_WIDTH), lambda b, m: (b, m, 0)),
        out_shape=jax.ShapeDtypeStruct((bsz, s, FOURIER_WIDTH), BF16),
        compiler_params=_params(("parallel", "parallel")),
        name="seqdft",
    )(cmat, smat, ab, ab)


def _attn_kernel(sink_ref, q_ref, kp_ref, kc_ref, kn_ref, vp_ref, vc_ref, vn_ref, bias_ref, o_ref, *, nblocks):
    n = pl.program_id(1)
    col = lax.broadcasted_iota(jnp.int32, (1, 3 * ATTN_BLOCK), 1)
    off_edge = ((col < ATTN_BLOCK) & (n == 0)) | ((col >= 2 * ATTN_BLOCK) & (n == nblocks - 1))
    edge = jnp.where(off_edge, NEG_INF, 0.0).astype(F32)
    kcat = jnp.concatenate([kp_ref[...], kc_ref[...], kn_ref[...]], axis=0)
    vcat = jnp.concatenate([vp_ref[...], vc_ref[...], vn_ref[...]], axis=0)
    vones = jnp.concatenate([vcat, jnp.ones_like(vcat)], axis=1)
    lane = lax.broadcasted_iota(jnp.int32, (ATTN_BLOCK, KV_WIDTH), 1)
    low = lane < HEAD_DIM
    for g in range(Q_GROUP):
        qg = q_ref[:, g * KV_WIDTH:(g + 1) * KV_WIDTH]
        halves = []
        for kh in range(N_KV_HEADS):
            h = kh * Q_GROUP + g
            qm = jnp.where(low if kh == 0 else jnp.logical_not(low), qg, jnp.zeros_like(qg))
            s = _dot_nt(qm, kcat) + bias_ref[h] + edge
            sink = sink_ref[h]
            m = jnp.maximum(jnp.max(s, axis=-1, keepdims=True), sink)
            p = jnp.exp(s - m)
            pv = _dot(p.astype(BF16), vones)
            halves.append(pv[:, :KV_WIDTH] / (pv[:, KV_WIDTH:] + jnp.exp(sink - m)))
        o_ref[:, g * KV_WIDTH:(g + 1) * KV_WIDTH] = jnp.where(low, halves[0], halves[1]).astype(BF16)


def _attention(q, k, v, bias, sink):
    bsz, s, _ = q.shape
    nb = s // ATTN_BLOCK
    kv_spec = lambda fn: pl.BlockSpec((None, ATTN_BLOCK, KV_WIDTH), fn)
    prev = lambda b, n: (b, jnp.maximum(n - 1, 0), 0)
    cur = lambda b, n: (b, n, 0)
    nxt = lambda b, n: (b, jnp.minimum(n + 1, nb - 1), 0)
    return pl.pallas_call(
        functools.partial(_attn_kernel, nblocks=nb),
        grid=(bsz, nb),
        in_specs=[
            pl.BlockSpec(memory_space=pltpu.SMEM),
            pl.BlockSpec((None, ATTN_BLOCK, ATTN_WIDTH), cur),
            kv_spec(prev), kv_spec(cur), kv_spec(nxt),
            kv_spec(prev), kv_spec(cur), kv_spec(nxt),
            pl.BlockSpec(bias.shape, lambda b, n: (0, 0, 0)),
        ],
        out_specs=pl.BlockSpec((None, ATTN_BLOCK, ATTN_WIDTH), cur),
        out_shape=jax.ShapeDtypeStruct((bsz, s, ATTN_WIDTH), BF16),
        compiler_params=_params(("parallel", "parallel")),
        name="window_attn",
    )(sink, q, k, k, k, v, v, v, bias)


def _merge_kernel(x_ref, y_ref, a_ref, gt_ref, wfo_ref, wao_ref, wout_ref, g_ref, b_ref, o_ref):
    f = _dot(y_ref[...], wfo_ref[...])
    a = _dot(a_ref[...], wao_ref[...])
    merged = gt_ref[:, :D_MODEL].astype(F32) * f + gt_ref[:, D_MODEL:].astype(F32) * a
    o = _dot(merged.astype(BF16), wout_ref[...])
    o_ref[...] = _layer_norm(DEEPNORM_ALPHA * x_ref[...] + o, g_ref[...], b_ref[...])


def _merge(x, y, a, gt, wfo, wao, wout, g, b, tile):
    t = x.shape[0]
    row = lambda width: pl.BlockSpec((tile, width), lambda i: (i, 0))
    full = lambda arr: pl.BlockSpec(arr.shape, lambda i: (0,) * arr.ndim)
    return pl.pallas_call(
        _merge_kernel,
        grid=(t // tile,),
        in_specs=[row(D_MODEL), row(FOURIER_WIDTH), row(ATTN_WIDTH), row(GATE_WIDTH),
                  full(wfo), full(wao), full(wout), full(g), full(b)],
        out_specs=row(D_MODEL),
        out_shape=jax.ShapeDtypeStruct((t, D_MODEL), F32),
        compiler_params=_params(("parallel",)),
        name="merge_ln1",
    )(x, y, a, gt, wfo, wao, wout, g, b)


def _extract_sorted(s, ids, count, vals_ref):
    def body(r, carry):
        s, rank = carry
        m = jnp.max(s, axis=0, keepdims=True)
        first = jnp.min(jnp.where(s == m, ids, _BIG_INDEX), axis=0, keepdims=True)
        sel = ids == first
        if vals_ref is not None:
            vals_ref[pl.ds(r, 1), :] = m
        rank = jnp.where(sel, lax.convert_element_type(r, F32), rank)
        s = jnp.where(sel, -jnp.inf, s)
        return s, rank

    rank0 = jnp.full(s.shape, float(count), F32)
    s, rank = lax.fori_loop(0, count, body, (s, rank0))
    return rank, s


def _select_exact(s1, s2, key_ids, cand_ids, cand_mask, v1_scr, v2_scr):
    rank1, _ = _extract_sorted(s1, key_ids, PEER_TOPK, v1_scr)
    rank2, _ = _extract_sorted(s2, key_ids, PEER_TOPK, v2_scr)
    v1 = v1_scr[...]
    v2 = v2_scr[...]
    groups = [v1[r1:r1 + 1] + v2[lo:lo + SUBLANES] for r1, lo in _CAND_GROUPS]
    groups.append(v1[SUBLANES:] + v2[0:1])
    cand = jnp.concatenate(groups, axis=0) + cand_mask
    crank, _ = _extract_sorted(cand, cand_ids, PEER_TOPK, None)
    chosen = crank < float(PEER_TOPK)
    pexp = jnp.where(chosen, jnp.exp(cand - cand[0:1]), 0.0)
    inv_z = 1.0 / jnp.sum(pexp, axis=0, keepdims=True)
    cnt = chosen.astype(F32)
    counts = [jnp.sum(cnt[0:2 * SUBLANES], axis=0, keepdims=True)]
    for gi in range(2, len(_CAND_GROUPS)):
        counts.append(jnp.sum(cnt[gi * SUBLANES:(gi + 1) * SUBLANES], axis=0, keepdims=True))
    tail = cnt[len(_CAND_GROUPS) * SUBLANES:]
    c1 = jnp.zeros_like(s1)
    for r in range(PEER_TOPK):
        cr = counts[r] if r < SUBLANES else tail[r - SUBLANES:r - SUBLANES + 1]
        c1 = jnp.where(rank1 == float(r), cr, c1)
    e1 = jnp.exp(s1 - v1[0:1]) * inv_z
    e2 = jnp.exp(s2 - v2[0:1])
    return e1, c1, e2, rank2


def _merge_exchange_pairs(n):
    pairs = []
    t = max(1, math.ceil(math.log2(n)))
    p = 1 << (t - 1)
    while p > 0:
        q, r, d = 1 << (t - 1), 0, p
        while d > 0:
            pairs.extend((i, i + d) for i in range(n - d) if (i & p) == r)
            d, q, r = q - p, q >> 1, p
        p >>= 1
    return pairs


def _compare_exchange(v, i, j):
    v[i], v[j] = jnp.maximum(v[i], v[j]), jnp.minimum(v[i], v[j])


def _sort_desc(v):
    v = list(v)
    for i, j in _merge_exchange_pairs(len(v)):
        _compare_exchange(v, i, j)
    return v


def _bitonic_merge(v):
    v = list(v)
    d = len(v) // 2
    while d:
        for k in range(len(v)):
            if not k & d:
                _compare_exchange(v, k, k + d)
        d //= 2
    return v


def _across_sublanes(x, op):
    for shift in (4, 2, 1):
        x = op(x, pltpu.roll(x, shift, 0))
    return x


def _top16_of_sublane_lists(lists):
    for shift in (4, 2, 1):
        other = [pltpu.roll(x, shift, 0) for x in lists]
        n = len(lists)
        merged = []
        for k in range(PEER_TOPK):
            mine = lists[k] if k < n else None
            theirs = other[PEER_TOPK - 1 - k] if PEER_TOPK - 1 - k < n else None
            merged.append(mine if theirs is None else theirs if mine is None else jnp.maximum(mine, theirs))
        lists = _bitonic_merge(merged)
    return lists


def _select_sorted(s1, s2, cand_mask):
    nv = PEER_NKEYS // SUBLANES
    a1 = [s1[j * SUBLANES:(j + 1) * SUBLANES] for j in range(nv)]
    a2 = [s2[j * SUBLANES:(j + 1) * SUBLANES] for j in range(nv)]
    v1 = _top16_of_sublane_lists(_sort_desc(a1))
    v2 = _top16_of_sublane_lists(_sort_desc(a2))
    sub = lax.broadcasted_iota(jnp.int32, (SUBLANES, LANES), 0)

    def by_sublane(vals):
        out = vals[SUBLANES - 1]
        for j in range(SUBLANES - 2, -1, -1):
            out = jnp.where(sub == j, vals[j], out)
        return out

    v2_nat = {0: by_sublane(v2[:SUBLANES]), SUBLANES: by_sublane(v2[SUBLANES:])}
    groups = [v1[r1] + v2_nat[lo] for r1, lo in _CAND_GROUPS]
    groups.append(by_sublane(v1[SUBLANES:]) + v2[0])
    groups = [g + cand_mask[i * SUBLANES:(i + 1) * SUBLANES] for i, g in enumerate(groups)]
    top = _top16_of_sublane_lists(_sort_desc(groups))
    thr = top[PEER_TOPK - 1]
    z = None
    for k in range(PEER_TOPK):
        term = jnp.exp(top[k] - top[0])
        z = term if z is None else z + term
    inv_z = 1.0 / z
    picked = [jnp.where(g >= thr, 1.0, 0.0) for g in groups[:-1]]
    add = lambda a, b: a + b
    counts = [_across_sublanes(picked[0] + picked[1], add)]
    counts += [_across_sublanes(picked[r + 1], add) for r in range(1, SUBLANES)]
    counts += [jnp.where(v1[r] + v2[0] >= thr, 1.0, 0.0) for r in range(SUBLANES, PEER_TOPK)]
    e1, c1, e2, r2 = [], [], [], []
    c_total = None
    r_total = None
    for j in range(nv):
        c = jnp.zeros((SUBLANES, LANES), F32)
        r = jnp.full((SUBLANES, LANES), float(PEER_TOPK), F32)
        for k in range(PEER_TOPK - 1, -1, -1):
            c = jnp.where(a1[j] >= v1[k], counts[k], c)
            r = jnp.where(a2[j] >= v2[k], float(k), r)
        c1.append(c)
        r2.append(r)
        e1.append(jnp.exp(a1[j] - v1[0]) * inv_z)
        e2.append(jnp.exp(a2[j] - v2[0]))
        c_total = c if c_total is None else c_total + c
        r_total = (float(PEER_TOPK) - r) if r_total is None else r_total + (float(PEER_TOPK) - r)
    c_total = _across_sublanes(c_total, add)
    r_total = _across_sublanes(r_total, add)
    distinct_total = float(PEER_TOPK * (PEER_TOPK + 1) // 2)
    bad = jnp.where((c_total != float(PEER_TOPK)) | (r_total != distinct_total), 1.0, 0.0)
    cat = lambda parts: jnp.concatenate(parts, axis=0)
    return cat(e1), cat(c1), cat(e2), cat(r2), bad


def _select_kernel(x_ref, pq_ref, keys_ref, meta_ref, xt_ref, e1_ref, c1_ref, e2_ref, r2_ref,
                   q_scr, v1_scr, v2_scr, *, tile):
    x = x_ref[...]
    xt_ref[...] = x.T.astype(BF16)
    q = _dot(x.astype(BF16), pq_ref[...])
    for hp in range(2 * PEER_HEADS):
        q_scr[hp] = q[:, hp * PEER_KEY_DIM:(hp + 1) * PEER_KEY_DIM].astype(BF16)

    chunks = [slice(c * LANES, (c + 1) * LANES) for c in range(tile // LANES)]

    def scores(h):
        return tuple(_dot_nt(keys_ref[h, half], q_scr[2 * h + half, tok, :]) for tok in chunks for half in (0, 1))

    def head_body(h, s_all):
        s_next = scores(jnp.minimum(h + 1, PEER_HEADS - 1))

        def emit(tok, e1, c1, e2, r2):
            e1_ref[h, :, tok] = e1
            c1_ref[h, :, tok] = c1
            e2_ref[h, :, tok] = e2.astype(BF16)
            r2_ref[h, :, tok] = r2.astype(BF16)

        flags = []
        for c, tok in enumerate(chunks):
            e1, c1, e2, r2, bad = _select_sorted(s_all[2 * c], s_all[2 * c + 1], meta_ref[1])
            emit(tok, e1, c1, e2, r2)
            flags.append(jnp.max(bad))
        for c, tok in enumerate(chunks):
            @pl.when(flags[c] > 0.0)
            def _():
                key_ids = lax.broadcasted_iota(jnp.int32, (PEER_NKEYS, LANES), 0).astype(F32)
                emit(tok, *_select_exact(s_all[2 * c], s_all[2 * c + 1], key_ids, meta_ref[0], meta_ref[1],
                                         v1_scr, v2_scr))
        return s_next

    lax.fori_loop(0, PEER_HEADS, head_body, scores(0))


def _cand_meta():
    ids = np.full((_CAND_ROWS, LANES), _BIG_INDEX, np.float32)
    mask = np.full((_CAND_ROWS, LANES), -np.inf, np.float32)
    for gi, (r1, lo) in enumerate(_CAND_GROUPS):
        for j in range(SUBLANES):
            r2 = lo + j
            if (r1 + 1) * (r2 + 1) <= PEER_TOPK:
                ids[gi * SUBLANES + j] = r1 * PEER_TOPK + r2
                mask[gi * SUBLANES + j] = 0.0
    base = len(_CAND_GROUPS) * SUBLANES
    for j in range(SUBLANES):
        ids[base + j] = (SUBLANES + j) * PEER_TOPK
        mask[base + j] = 0.0
    return np.stack([ids, mask])


def _peer_select(x, pq, keys, tile):
    t = x.shape[0]
    meta = jnp.asarray(_cand_meta())
    dense = lambda dtype: jax.ShapeDtypeStruct((PEER_HEADS, PEER_NKEYS, t), dtype)
    dense_spec = pl.BlockSpec((PEER_HEADS, PEER_NKEYS, tile), lambda i: (0, 0, i))
    full = lambda arr: pl.BlockSpec(arr.shape, lambda i: (0,) * arr.ndim)
    return pl.pallas_call(
        functools.partial(_select_kernel, tile=tile),
        grid=(t // tile,),
        in_specs=[pl.BlockSpec((tile, D_MODEL), lambda i: (i, 0)), full(pq), full(keys), full(meta)],
        out_specs=[pl.BlockSpec((D_MODEL, tile), lambda i: (0, i))] + [dense_spec] * 4,
        out_shape=[jax.ShapeDtypeStruct((D_MODEL, t), BF16), dense(F32), dense(F32), dense(BF16), dense(BF16)],
        scratch_shapes=[
            pltpu.VMEM((2 * PEER_HEADS, tile, PEER_KEY_DIM), BF16),
            pltpu.VMEM((PEER_TOPK, LANES), F32),
            pltpu.VMEM((PEER_TOPK, LANES), F32),
        ],
        compiler_params=_params(("parallel",)),
        name="peer_select",
    )(x, pq, keys, meta)


def _activations(xt_ref, u_ref, act_ref):
    z = _dot(u_ref[...], xt_ref[...])
    half = z * (1.0 / math.sqrt(2.0))
    act_ref[...] = (half + half * lax.erf(z)).astype(BF16)


def _gate_activations(block, e1_ref, c1_ref, e2_ref, r2_ref, act_ref, w_ref, rows_per_block):
    tile = act_ref.shape[1]
    for jj in range(rows_per_block):
        j = block * rows_per_block + jj
        c1 = [jnp.broadcast_to(c1_ref[h, pl.ds(j, 1), :], (BF16_ROWS, tile)).astype(BF16)
              for h in range(PEER_HEADS)]
        e1 = [jnp.broadcast_to(e1_ref[h, pl.ds(j, 1), :], (BF16_ROWS, tile)).astype(BF16)
              for h in range(PEER_HEADS)]
        for g in range(PEER_NKEYS // BF16_ROWS):
            keys = slice(g * BF16_ROWS, (g + 1) * BF16_ROWS)
            gate = None
            for h in range(PEER_HEADS):
                picked = jnp.where(r2_ref[h, keys, :] < c1[h], e2_ref[h, keys, :], jnp.zeros((), BF16))
                term = picked * e1[h]
                gate = term if gate is None else gate + term
            rows = slice(jj * PEER_NKEYS + g * BF16_ROWS, jj * PEER_NKEYS + (g + 1) * BF16_ROWS)
            w_ref[rows, :] = gate * act_ref[rows, :]


def _dense_kernel(xt_ref, e1_ref, c1_ref, e2_ref, r2_ref, u_ref, vtp_ref, *refs, rows_per_block, nb):
    vt_refs, o_ref = refs[:nb], refs[nb]
    act_scr, w_scr = refs[nb + 1:2 * nb + 1], refs[2 * nb + 1:]
    e = pl.program_id(1)
    nrows = rows_per_block * PEER_NKEYS

    @pl.when(e == 0)
    def _():
        o_ref[...] = jnp.zeros_like(o_ref)
        w_scr[nb - 1][...] = jnp.zeros_like(w_scr[nb - 1])

    def activations(i):
        _activations(xt_ref, u_ref.at[i * nrows:(i + 1) * nrows, :], act_scr[i])

    activations(0)
    o_ref[...] += _dot(vtp_ref[...], w_scr[nb - 1][...])
    for i in range(nb):
        _gate_activations(nb * e + i, e1_ref, c1_ref, e2_ref, r2_ref, act_scr[i], w_scr[i], rows_per_block)
        if i + 1 < nb:
            activations(i + 1)
            o_ref[...] += _dot(vt_refs[i][...], w_scr[i][...])

    @pl.when(e == pl.num_programs(1) - 1)
    def _():
        o_ref[...] += _dot(vt_refs[nb - 1][...], w_scr[nb - 1][...])


def _peer_dense(xt, e1, c1, e2, r2, u, vt, tile, eblock, nb):
    t = xt.shape[1]
    nsteps = PEER_N_EXPERTS // (nb * eblock)
    row_spec = pl.BlockSpec((PEER_HEADS, PEER_NKEYS, tile), lambda i, e: (0, 0, i))
    vt_spec = lambda fn: pl.BlockSpec((D_MODEL, eblock), fn)

    def vt_block(i, e, k):
        block = nb * e + k
        return (0, jnp.where(e == nsteps - 1, block, 0) if k == nb - 1 else block)

    return pl.pallas_call(
        functools.partial(_dense_kernel, rows_per_block=eblock // PEER_NKEYS, nb=nb),
        grid=(t // tile, nsteps),
        in_specs=[pl.BlockSpec((D_MODEL, tile), lambda i, e: (0, i))] + [row_spec] * 4 + [
            pl.BlockSpec((nb * eblock, D_MODEL), lambda i, e: (e, 0)),
            vt_spec(lambda i, e: (0, jnp.maximum(nb * e - 1, 0))),
        ] + [vt_spec(functools.partial(vt_block, k=k)) for k in range(nb)],
        out_specs=pl.BlockSpec((D_MODEL, tile), lambda i, e: (0, i)),
        out_shape=jax.ShapeDtypeStruct((D_MODEL, t), F32),
        scratch_shapes=[pltpu.VMEM((eblock, tile), BF16)] * (2 * nb),
        compiler_params=_params(("parallel", "arbitrary")),
        name="peer_dense",
    )(xt, e1, c1, e2, r2, u, vt, *([vt] * nb))


def _final_kernel(x_ref, cmt_ref, p_ref, wg_ref, wp_ref, g_ref, b_ref, o_ref):
    x = x_ref[...]
    gate = jax.nn.sigmoid(_dot(x.astype(BF16), wg_ref[...]))
    ple = gate * _dot(p_ref[...].astype(BF16), wp_ref[...])
    o_ref[...] = _layer_norm(DEEPNORM_ALPHA * x + cmt_ref[...].T + ple, g_ref[...], b_ref[...])


def _final(x, cmt, p_all, layer, wg, wp, g, b, tile):
    t = x.shape[0]
    full = lambda arr: pl.BlockSpec(arr.shape, lambda i: (0,) * arr.ndim)
    return pl.pallas_call(
        _final_kernel,
        grid=(t // tile,),
        in_specs=[pl.BlockSpec((tile, D_MODEL), lambda i: (i, 0)),
                  pl.BlockSpec((D_MODEL, tile), lambda i: (0, i)),
                  pl.BlockSpec((None, tile, PLE_DIM), lambda i: (layer, i, 0)),
                  full(wg), full(wp), full(g), full(b)],
        out_specs=pl.BlockSpec((tile, D_MODEL), lambda i: (i, 0)),
        out_shape=jax.ShapeDtypeStruct((t, D_MODEL), F32),
        compiler_params=_params(("parallel",)),
        name="ple_ln2",
    )(x, cmt, p_all, wg, wp, g, b)


def _channel_dft():
    c = np.arange(FOURIER_GROUP_DIM)
    ang = 2.0 * np.pi * ((c[:, None] * c[None, :]) % FOURIER_GROUP_DIM) / FOURIER_GROUP_DIM
    eye = np.eye(N_FOURIER_GROUPS)
    return np.concatenate([np.kron(eye, np.cos(ang)), np.kron(eye, -np.sin(ang))], axis=1)


def _sequence_dft(s, scale):
    s_lo = 64
    s_hi = s // s_lo
    k = np.arange(s)
    ang_hi = 2.0 * np.pi * ((k[:, None] * np.arange(s_hi)[None, :]) % s_hi) / s_hi
    ang_lo = 2.0 * np.pi * ((k[:, None] * np.arange(s_lo)[None, :]) % s) / s
    ch = jnp.asarray(np.cos(ang_hi) * scale, F32)[:, :, None]
    sh = jnp.asarray(np.sin(ang_hi) * scale, F32)[:, :, None]
    cl = jnp.asarray(np.cos(ang_lo), F32)[:, None, :]
    sl = jnp.asarray(np.sin(ang_lo), F32)[:, None, :]
    cmat = (ch * cl - sh * sl).reshape(s, s).astype(BF16)
    smat = (sh * cl + ch * sl).reshape(s, s).astype(BF16)
    return cmat, smat


def _attn_bias():
    qi = np.arange(ATTN_BLOCK)
    kj = np.arange(3 * ATTN_BLOCK)
    rel = np.abs(qi[:, None] + ATTN_BLOCK - kj[None, :]).astype(np.float64)
    slopes = np.exp2(-8.0 * np.arange(1, N_HEADS + 1) / N_HEADS)
    bias = np.where(rel[None] <= WINDOW, -slopes[:, None, None] * rel[None], NEG_INF)
    return bias.astype(np.float32)


def _pick_tile(n, pref):
    tile = min(n, pref)
    assert n % tile == 0, (n, tile)
    return tile


def _tiles(t, s):
    return dict(
        rows=_pick_tile(t, 512),
        dft_rows=_pick_tile(s, 256),
        select_tokens=_pick_tile(t, 2 * LANES),
        dense_tokens=_pick_tile(t, 512),
        dense_experts=512,
        dense_blocks=2,
    )


def _trunk(x, p_all, consts, emb_ln, layers):
    bsz, s, _ = x.shape
    t = bsz * s
    mch, cmat, smat, bias = consts
    tiles = _tiles(t, s)
    x = x.reshape(t, D_MODEL)
    p_all = p_all.reshape(p_all.shape[0], t, PLE_DIM)
    for i, lw in enumerate(layers):
        if i == 0:
            x, ab, q, k, v, gt = _inproj(x, emb_ln, lw["w_in"], mch, tiles["rows"])
        else:
            ab, q, k, v, gt = _inproj(x, None, lw["w_in"], mch, tiles["rows"])
        y = _seqdft(ab.reshape(bsz, s, 2 * FOURIER_WIDTH), cmat, smat, tiles["dft_rows"])
        a = _attention(q.reshape(bsz, s, ATTN_WIDTH), k.reshape(bsz, s, KV_WIDTH),
                       v.reshape(bsz, s, KV_WIDTH), bias, lw["sink"])
        x = _merge(x, y.reshape(t, FOURIER_WIDTH), a.reshape(t, ATTN_WIDTH), gt,
                   lw["w_fo"], lw["w_ao"], lw["w_out"], lw["ln1_g"], lw["ln1_b"], tiles["rows"])
        xt, e1, c1, e2, r2 = _peer_select(x, lw["pq"], lw["keys"], tiles["select_tokens"])
        cmt = _peer_dense(xt, e1, c1, e2, r2, lw["u"], lw["vt"], tiles["dense_tokens"],
                          tiles["dense_experts"], tiles["dense_blocks"])
        x = _final(x, cmt, p_all, i, lw["wg"], lw["wp"], lw["ln2_g"], lw["ln2_b"], tiles["rows"])
    return x.reshape(bsz, s, D_MODEL)


def _prepare_layer(i, w_in, attn_sink, w_fourier_out, w_attn_out, w_out, ln1_g, ln1_b, peer_w_q,
                   peer_keys, peer_u, peer_v, ple_w_gate, ple_w_proj, ln2_g, ln2_b):
    w = w_in[i]
    wq = w[:, _O_Q:_O_K].reshape(D_MODEL, N_KV_HEADS, Q_GROUP, HEAD_DIM).transpose(0, 2, 1, 3)
    wq = wq.reshape(D_MODEL, ATTN_WIDTH) * (HEAD_DIM ** -0.5)
    w_perm = jnp.concatenate([w[:, :_O_Q], wq, w[:, _O_K:]], axis=1).astype(BF16)
    wao = w_attn_out[i].reshape(N_KV_HEADS, Q_GROUP, HEAD_DIM, D_MODEL).transpose(1, 0, 2, 3)
    row = lambda a: a[i].reshape(1, D_MODEL).astype(F32)
    return dict(
        w_in=w_perm, sink=attn_sink[i].astype(F32),
        w_fo=w_fourier_out[i].astype(BF16), w_ao=wao.reshape(ATTN_WIDTH, D_MODEL).astype(BF16),
        w_out=w_out[i].astype(BF16), ln1_g=row(ln1_g), ln1_b=row(ln1_b),
        pq=peer_w_q[i].astype(BF16), keys=peer_keys[i].astype(BF16),
        u=(peer_u[i] * (1.0 / math.sqrt(2.0))).astype(BF16), vt=peer_v[i].astype(BF16).T,
        wg=ple_w_gate[i].astype(BF16), wp=ple_w_proj[i].astype(BF16), ln2_g=row(ln2_g), ln2_b=row(ln2_b),
    )


def kernel(x_prompt, x_sample, p_prompt, p_sample, emb_ln_g, emb_ln_b, w_in, attn_sink, w_fourier_out, w_attn_out, w_out, ln1_g, ln1_b, peer_w_q, peer_keys, peer_u, peer_v, ple_w_gate, ple_w_proj, ln2_g, ln2_b):
    depth = w_in.shape[0]
    layers = [_prepare_layer(i, w_in, attn_sink, w_fourier_out, w_attn_out, w_out, ln1_g, ln1_b, peer_w_q,
                             peer_keys, peer_u, peer_v, ple_w_gate, ple_w_proj, ln2_g, ln2_b)
              for i in range(depth)]
    emb_ln = (emb_ln_g.reshape(1, D_MODEL).astype(F32), emb_ln_b.reshape(1, D_MODEL).astype(F32))
    mch = jnp.asarray(_channel_dft() / math.sqrt(FOURIER_GROUP_DIM), BF16)
    bias = jnp.asarray(_attn_bias())
    outs = []
    for x, p in ((x_prompt, p_prompt), (x_sample, p_sample)):
        s = x.shape[1]
        cmat, smat = _sequence_dft(s, 1.0 / math.sqrt(s))
        outs.append(_trunk(x, p, (mch, cmat, smat, bias), emb_ln, layers))
    return tuple(outs)
```

```python
import functools
import math

import numpy as np
import jax
import jax.numpy as jnp
from jax import lax
from jax.experimental import pallas as pl
from jax.experimental.pallas import tpu as pltpu

F32 = jnp.float32
BF16 = jnp.bfloat16

D_MODEL = 1024
N_HEADS = 8
N_KV_HEADS = 2
Q_GROUP = N_HEADS // N_KV_HEADS
HEAD_DIM = 64
WINDOW = 128
ATTN_BLOCK = 128
ATTN_WIDTH = N_HEADS * HEAD_DIM
KV_WIDTH = N_KV_HEADS * HEAD_DIM
N_FOURIER_GROUPS = 8
FOURIER_GROUP_DIM = 64
FOURIER_WIDTH = N_FOURIER_GROUPS * FOURIER_GROUP_DIM
GATE_WIDTH = 2 * D_MODEL
IN_WIDTH = FOURIER_WIDTH + ATTN_WIDTH + 2 * KV_WIDTH + GATE_WIDTH
PEER_HEADS = 8
PEER_NKEYS = 128
PEER_N_EXPERTS = PEER_NKEYS * PEER_NKEYS
PEER_TOPK = 16
PEER_KEY_DIM = 128
PLE_DIM = 256
DEPTH = 2
DEEPNORM_ALPHA = (2 * DEPTH) ** 0.25
LN_EPS = 1e-5
NEG_INF = -1e30

LANES = 128
SUBLANES = 8
BF16_ROWS = 16
VMEM_LIMIT = 52 * 1024 * 1024

_O_F = 0
_O_Q = _O_F + FOURIER_WIDTH
_O_K = _O_Q + ATTN_WIDTH
_O_V = _O_K + KV_WIDTH
_O_G = _O_V + KV_WIDTH

_CAND_GROUPS = [(0, 0), (0, 8), (1, 0), (2, 0), (3, 0), (4, 0), (5, 0), (6, 0), (7, 0)]
_CAND_ROWS = 8 * (len(_CAND_GROUPS) + 1)
_BIG_INDEX = 1.0e6


def _params(semantics, flags=None):
    return pltpu.CompilerParams(dimension_semantics=semantics, vmem_limit_bytes=VMEM_LIMIT, flags=flags)


def _layer_norm(x, g, b):
    mu = jnp.mean(x, axis=-1, keepdims=True)
    xc = x - mu
    var = jnp.mean(xc * xc, axis=-1, keepdims=True)
    return xc * lax.rsqrt(var + LN_EPS) * g + b


def _dot(a, b):
    return jnp.dot(a, b, preferred_element_type=F32)


def _dot_nt(a, b):
    return lax.dot_general(a, b, (((1,), (1,)), ((), ())), preferred_element_type=F32)


def _inproj_kernel(*refs, apply_ln):
    if apply_ln:
        x_ref, g_ref, b_ref, w_ref, mch_ref, xn_ref, ab_ref, q_ref, k_ref, v_ref, gt_ref = refs
        x = _layer_norm(x_ref[...], g_ref[...], b_ref[...])
        xn_ref[...] = x
    else:
        x_ref, w_ref, mch_ref, ab_ref, q_ref, k_ref, v_ref, gt_ref = refs
        x = x_ref[...]
    xb = x.astype(BF16)
    f = _dot(xb, w_ref[:, _O_F:_O_Q])
    ab_ref[...] = _dot(f.astype(BF16), mch_ref[...]).astype(BF16)
    q_ref[...] = _dot(xb, w_ref[:, _O_Q:_O_K]).astype(BF16)
    k_ref[...] = _dot(xb, w_ref[:, _O_K:_O_V]).astype(BF16)
    v_ref[...] = _dot(xb, w_ref[:, _O_V:_O_G]).astype(BF16)
    gt_ref[...] = jax.nn.sigmoid(_dot(xb, w_ref[:, _O_G:IN_WIDTH])).astype(BF16)


def _inproj(x, ln, w_in, mch, tile):
    t = x.shape[0]
    apply_ln = ln is not None
    row = lambda width: pl.BlockSpec((tile, width), lambda i: (i, 0))
    full = lambda a: pl.BlockSpec(a.shape, lambda i: (0,) * a.ndim)
    ins = [x] + ([ln[0], ln[1]] if apply_ln else []) + [w_in, mch]
    in_specs = [row(D_MODEL)] + ([full(ln[0]), full(ln[1])] if apply_ln else []) + [full(w_in), full(mch)]
    widths = [2 * FOURIER_WIDTH, ATTN_WIDTH, KV_WIDTH, KV_WIDTH, GATE_WIDTH]
    out_shape = [jax.ShapeDtypeStruct((t, w), BF16) for w in widths]
    out_specs = [row(w) for w in widths]
    if apply_ln:
        out_shape = [jax.ShapeDtypeStruct((t, D_MODEL), F32)] + out_shape
        out_specs = [row(D_MODEL)] + out_specs
    return pl.pallas_call(
        functools.partial(_inproj_kernel, apply_ln=apply_ln),
        grid=(t // tile,),
        in_specs=in_specs,
        out_specs=out_specs,
        out_shape=out_shape,
        compiler_params=_params(("parallel",)),
        name="inproj_ln" if apply_ln else "inproj",
    )(*ins)


def _seqdft_kernel(c_ref, s_ref, a_ref, b_ref, y_ref):
    y_ref[...] = (_dot(c_ref[...], a_ref[...]) + _dot(s_ref[...], b_ref[...])).astype(BF16)


def _seqdft(ab, cmat, smat, tile):
    bsz, s, _ = ab.shape
    return pl.pallas_call(
        _seqdft_kernel,
        grid=(bsz, s // tile),
        in_specs=[
            pl.BlockSpec((tile, s), lambda b, m: (m, 0)),
            pl.BlockSpec((tile, s), lambda b, m: (m, 0)),
            pl.BlockSpec((None, s, FOURIER_WIDTH), lambda b, m: (b, 0, 0)),
            pl.BlockSpec((None, s, FOURIER_WIDTH), lambda b, m: (b, 0, 1)),
        ],
        out_specs=pl.BlockSpec((None, tile, FOURIER_WIDTH), lambda b, m: (b, m, 0)),
        out_shape=jax.ShapeDtypeStruct((bsz, s, FOURIER_WIDTH), BF16),
        compiler_params=_params(("parallel", "parallel")),
        name="seqdft",
    )(cmat, smat, ab, ab)


def _attn_kernel(sink_ref, q_ref, kp_ref, kc_ref, kn_ref, vp_ref, vc_ref, vn_ref, bias_ref, o_ref, *, nsteps, qblocks):
    n = pl.program_id(1)
    col = lax.broadcasted_iota(jnp.int32, (1, 3 * ATTN_BLOCK), 1)
    kall = jnp.concatenate([kp_ref[...], kc_ref[...], kn_ref[...]], axis=0)
    vall = jnp.concatenate([vp_ref[...], vc_ref[...], vn_ref[...]], axis=0)
    vones = jnp.concatenate([vall, jnp.ones_like(vall)], axis=1)
    lane = lax.broadcasted_iota(jnp.int32, (ATTN_BLOCK, KV_WIDTH), 1)
    low = lane < HEAD_DIM
    for j in range(qblocks):
        rows = slice(j * ATTN_BLOCK, (j + 1) * ATTN_BLOCK)
        window = slice(j * ATTN_BLOCK, (j + 3) * ATTN_BLOCK)
        off_edge = None
        if j == 0:
            off_edge = (col < ATTN_BLOCK) & (n == 0)
        if j == qblocks - 1:
            after = (col >= 2 * ATTN_BLOCK) & (n == nsteps - 1)
            off_edge = after if off_edge is None else off_edge | after
        kcat = kall[window]
        vcat = vones[window]
        for g in range(Q_GROUP):
            qg = q_ref[rows, g * KV_WIDTH:(g + 1) * KV_WIDTH]
            halves = []
            for kh in range(N_KV_HEADS):
                h = kh * Q_GROUP + g
                qm = jnp.where(low if kh == 0 else jnp.logical_not(low), qg, jnp.zeros_like(qg))
                s = _dot_nt(qm, kcat) + bias_ref[h]
                if off_edge is not None:
                    s = s + jnp.where(off_edge, NEG_INF, 0.0).astype(F32)
                sink = sink_ref[h]
                m = jnp.maximum(jnp.max(s, axis=-1, keepdims=True), sink)
                p = jnp.exp(s - m)
                pv = _dot(p.astype(BF16), vcat)
                halves.append(pv[:, :KV_WIDTH] / (pv[:, KV_WIDTH:] + jnp.exp(sink - m)))
            o_ref[rows, g * KV_WIDTH:(g + 1) * KV_WIDTH] = jnp.where(low, halves[0], halves[1]).astype(BF16)


def _attention(q, k, v, bias, sink, qblocks):
    bsz, s, _ = q.shape
    nb = s // ATTN_BLOCK
    nsteps = nb // qblocks
    edge_spec = lambda fn: pl.BlockSpec((None, ATTN_BLOCK, KV_WIDTH), fn)
    prev = lambda b, n: (b, jnp.maximum(qblocks * n - 1, 0), 0)
    cur = lambda b, n: (b, n, 0)
    nxt = lambda b, n: (b, jnp.minimum(qblocks * (n + 1), nb - 1), 0)
    mid_spec = pl.BlockSpec((None, qblocks * ATTN_BLOCK, KV_WIDTH), cur)
    return pl.pallas_call(
        functools.partial(_attn_kernel, nsteps=nsteps, qblocks=qblocks),
        grid=(bsz, nsteps),
        in_specs=[
            pl.BlockSpec(memory_space=pltpu.SMEM),
            pl.BlockSpec((None, qblocks * ATTN_BLOCK, ATTN_WIDTH), cur),
            edge_spec(prev), mid_spec, edge_spec(nxt),
            edge_spec(prev), mid_spec, edge_spec(nxt),
            pl.BlockSpec(bias.shape, lambda b, n: (0, 0, 0)),
        ],
        out_specs=pl.BlockSpec((None, qblocks * ATTN_BLOCK, ATTN_WIDTH), cur),
        out_shape=jax.ShapeDtypeStruct((bsz, s, ATTN_WIDTH), BF16),
        compiler_params=_params(("parallel", "parallel")),
        name="window_attn",
    )(sink, q, k, k, k, v, v, v, bias)


def _merge_kernel(x_ref, y_ref, a_ref, gt_ref, wfo_ref, wao_ref, wout_ref, g_ref, b_ref, o_ref):
    f = _dot(y_ref[...], wfo_ref[...])
    a = _dot(a_ref[...], wao_ref[...])
    merged = gt_ref[:, :D_MODEL].astype(F32) * f + gt_ref[:, D_MODEL:].astype(F32) * a
    o = _dot(merged.astype(BF16), wout_ref[...])
    o_ref[...] = _layer_norm(DEEPNORM_ALPHA * x_ref[...] + o, g_ref[...], b_ref[...])


def _merge(x, y, a, gt, wfo, wao, wout, g, b, tile):
    t = x.shape[0]
    row = lambda width: pl.BlockSpec((tile, width), lambda i: (i, 0))
    full = lambda arr: pl.BlockSpec(arr.shape, lambda i: (0,) * arr.ndim)
    return pl.pallas_call(
        _merge_kernel,
        grid=(t // tile,),
        in_specs=[row(D_MODEL), row(FOURIER_WIDTH), row(ATTN_WIDTH), row(GATE_WIDTH),
                  full(wfo), full(wao), full(wout), full(g), full(b)],
        out_specs=row(D_MODEL),
        out_shape=jax.ShapeDtypeStruct((t, D_MODEL), F32),
        compiler_params=_params(("parallel",)),
        name="merge_ln1",
    )(x, y, a, gt, wfo, wao, wout, g, b)


def _extract_sorted(s, ids, count, vals_ref):
    def body(r, carry):
        s, rank = carry
        m = jnp.max(s, axis=0, keepdims=True)
        first = jnp.min(jnp.where(s == m, ids, _BIG_INDEX), axis=0, keepdims=True)
        sel = ids == first
        if vals_ref is not None:
            vals_ref[pl.ds(r, 1), :] = m
        rank = jnp.where(sel, lax.convert_element_type(r, F32), rank)
        s = jnp.where(sel, -jnp.inf, s)
        return s, rank

    rank0 = jnp.full(s.shape, float(count), F32)
    s, rank = lax.fori_loop(0, count, body, (s, rank0))
    return rank, s


def _select_exact(s1, s2, key_ids, cand_ids, cand_mask, v1_scr, v2_scr):
    rank1, _ = _extract_sorted(s1, key_ids, PEER_TOPK, v1_scr)
    rank2, _ = _extract_sorted(s2, key_ids, PEER_TOPK, v2_scr)
    v1 = v1_scr[...]
    v2 = v2_scr[...]
    groups = [v1[r1:r1 + 1] + v2[lo:lo + SUBLANES] for r1, lo in _CAND_GROUPS]
    groups.append(v1[SUBLANES:] + v2[0:1])
    cand = jnp.concatenate(groups, axis=0) + cand_mask
    crank, _ = _extract_sorted(cand, cand_ids, PEER_TOPK, None)
    chosen = crank < float(PEER_TOPK)
    pexp = jnp.where(chosen, jnp.exp(cand - cand[0:1]), 0.0)
    inv_z = 1.0 / jnp.sum(pexp, axis=0, keepdims=True)
    cnt = chosen.astype(F32)
    counts = [jnp.sum(cnt[0:2 * SUBLANES], axis=0, keepdims=True)]
    for gi in range(2, len(_CAND_GROUPS)):
        counts.append(jnp.sum(cnt[gi * SUBLANES:(gi + 1) * SUBLANES], axis=0, keepdims=True))
    tail = cnt[len(_CAND_GROUPS) * SUBLANES:]
    c1 = jnp.zeros_like(s1)
    for r in range(PEER_TOPK):
        cr = counts[r] if r < SUBLANES else tail[r - SUBLANES:r - SUBLANES + 1]
        c1 = jnp.where(rank1 == float(r), cr, c1)
    e1 = jnp.exp(s1 - v1[0:1]) * inv_z
    e2 = jnp.exp(s2 - v2[0:1])
    return e1, c1, e2, rank2


def _merge_exchange_pairs(n):
    pairs = []
    t = max(1, math.ceil(math.log2(n)))
    p = 1 << (t - 1)
    while p > 0:
        q, r, d = 1 << (t - 1), 0, p
        while d > 0:
            pairs.extend((i, i + d) for i in range(n - d) if (i & p) == r)
            d, q, r = q - p, q >> 1, p
        p >>= 1
    return pairs


def _compare_exchange(v, i, j):
    v[i], v[j] = jnp.maximum(v[i], v[j]), jnp.minimum(v[i], v[j])


def _sort_desc(v):
    v = list(v)
    for i, j in _merge_exchange_pairs(len(v)):
        _compare_exchange(v, i, j)
    return v


def _bitonic_merge(v):
    v = list(v)
    d = len(v) // 2
    while d:
        for k in range(len(v)):
            if not k & d:
                _compare_exchange(v, k, k + d)
        d //= 2
    return v


def _across_sublanes(x, op):
    for shift in (4, 2, 1):
        x = op(x, pltpu.roll(x, shift, 0))
    return x


def _top16_of_sublane_lists(lists):
    for shift in (4, 2, 1):
        other = [pltpu.roll(x, shift, 0) for x in lists]
        n = len(lists)
        merged = []
        for k in range(PEER_TOPK):
            mine = lists[k] if k < n else None
            theirs = other[PEER_TOPK - 1 - k] if PEER_TOPK - 1 - k < n else None
            merged.append(mine if theirs is None else theirs if mine is None else jnp.maximum(mine, theirs))
        lists = _bitonic_merge(merged)
    return lists


def _select_sorted(s1, s2, cand_mask):
    nv = PEER_NKEYS // SUBLANES
    a1 = [s1[j * SUBLANES:(j + 1) * SUBLANES] for j in range(nv)]
    a2 = [s2[j * SUBLANES:(j + 1) * SUBLANES] for j in range(nv)]
    v1 = _top16_of_sublane_lists(_sort_desc(a1))
    v2 = _top16_of_sublane_lists(_sort_desc(a2))
    sub = lax.broadcasted_iota(jnp.int32, (SUBLANES, LANES), 0)

    def by_sublane(vals):
        out = vals[SUBLANES - 1]
        for j in range(SUBLANES - 2, -1, -1):
            out = jnp.where(sub == j, vals[j], out)
        return out

    v2_nat = {0: by_sublane(v2[:SUBLANES]), SUBLANES: by_sublane(v2[SUBLANES:])}
    groups = [v1[r1] + v2_nat[lo] for r1, lo in _CAND_GROUPS]
    groups.append(by_sublane(v1[SUBLANES:]) + v2[0])
    groups = [g + cand_mask[i * SUBLANES:(i + 1) * SUBLANES] for i, g in enumerate(groups)]
    top = _top16_of_sublane_lists(_sort_desc(groups))
    thr = top[PEER_TOPK - 1]
    z = None
    for k in range(PEER_TOPK):
        term = jnp.exp(top[k] - top[0])
        z = term if z is None else z + term
    inv_z = 1.0 / z
    picked = [jnp.where(g >= thr, 1.0, 0.0) for g in groups[:-1]]
    add = lambda a, b: a + b
    counts = [_across_sublanes(picked[0] + picked[1], add)]
    counts += [_across_sublanes(picked[r + 1], add) for r in range(1, SUBLANES)]
    counts += [jnp.where(v1[r] + v2[0] >= thr, 1.0, 0.0) for r in range(SUBLANES, PEER_TOPK)]
    e1, c1, e2, r2 = [], [], [], []
    c_total = None
    r_total = None
    for j in range(nv):
        c = jnp.zeros((SUBLANES, LANES), F32)
        r = jnp.full((SUBLANES, LANES), float(PEER_TOPK), F32)
        for k in range(PEER_TOPK - 1, -1, -1):
            c = jnp.where(a1[j] >= v1[k], counts[k], c)
            r = jnp.where(a2[j] >= v2[k], float(k), r)
        c1.append(c)
        r2.append(r)
        e1.append(jnp.exp(a1[j] - v1[0]) * inv_z)
        e2.append(jnp.exp(a2[j] - v2[0]))
        c_total = c if c_total is None else c_total + c
        r_total = (float(PEER_TOPK) - r) if r_total is None else r_total + (float(PEER_TOPK) - r)
    c_total = _across_sublanes(c_total, add)
    r_total = _across_sublanes(r_total, add)
    distinct_total = float(PEER_TOPK * (PEER_TOPK + 1) // 2)
    bad = jnp.where((c_total != float(PEER_TOPK)) | (r_total != distinct_total), 1.0, 0.0)
    cat = lambda parts: jnp.concatenate(parts, axis=0)
    return cat(e1), cat(c1), cat(e2), cat(r2), bad


def _select_kernel(x_ref, pq_ref, keys_ref, meta_ref, xt_ref, e1_ref, c1_ref, e2_ref, r2_ref,
                   q_scr, v1_scr, v2_scr, *, tile):
    x = x_ref[...]
    xt_ref[...] = x.T.astype(BF16)
    q = _dot(x.astype(BF16), pq_ref[...])
    for hp in range(2 * PEER_HEADS):
        q_scr[hp] = q[:, hp * PEER_KEY_DIM:(hp + 1) * PEER_KEY_DIM].astype(BF16)

    chunks = [slice(c * LANES, (c + 1) * LANES) for c in range(tile // LANES)]

    def scores(h):
        return tuple(_dot_nt(keys_ref[h, half], q_scr[2 * h + half, tok, :]) for tok in chunks for half in (0, 1))

    def head_body(h, s_all):
        s_next = scores(jnp.minimum(h + 1, PEER_HEADS - 1))

        def emit(tok, e1, c1, e2, r2):
            e1_ref[h, :, tok] = e1
            c1_ref[h, :, tok] = c1
            e2_ref[h, :, tok] = e2.astype(BF16)
            r2_ref[h, :, tok] = r2.astype(BF16)

        flags = []
        for c, tok in enumerate(chunks):
            e1, c1, e2, r2, bad = _select_sorted(s_all[2 * c], s_all[2 * c + 1], meta_ref[1])
            emit(tok, e1, c1, e2, r2)
            flags.append(jnp.max(bad))
        for c, tok in enumerate(chunks):
            @pl.when(flags[c] > 0.0)
            def _():
                key_ids = lax.broadcasted_iota(jnp.int32, (PEER_NKEYS, LANES), 0).astype(F32)
                emit(tok, *_select_exact(s_all[2 * c], s_all[2 * c + 1], key_ids, meta_ref[0], meta_ref[1],
                                         v1_scr, v2_scr))
        return s_next

    lax.fori_loop(0, PEER_HEADS, head_body, scores(0))


def _cand_meta():
    ids = np.full((_CAND_ROWS, LANES), _BIG_INDEX, np.float32)
    mask = np.full((_CAND_ROWS, LANES), -np.inf, np.float32)
    for gi, (r1, lo) in enumerate(_CAND_GROUPS):
        for j in range(SUBLANES):
            r2 = lo + j
            if (r1 + 1) * (r2 + 1) <= PEER_TOPK:
                ids[gi * SUBLANES + j] = r1 * PEER_TOPK + r2
                mask[gi * SUBLANES + j] = 0.0
    base = len(_CAND_GROUPS) * SUBLANES
    for j in range(SUBLANES):
        ids[base + j] = (SUBLANES + j) * PEER_TOPK
        mask[base + j] = 0.0
    return np.stack([ids, mask])


def _peer_select(x, pq, keys, tile):
    t = x.shape[0]
    meta = jnp.asarray(_cand_meta())
    dense = lambda dtype: jax.ShapeDtypeStruct((PEER_HEADS, PEER_NKEYS, t), dtype)
    dense_spec = pl.BlockSpec((PEER_HEADS, PEER_NKEYS, tile), lambda i: (0, 0, i))
    full = lambda arr: pl.BlockSpec(arr.shape, lambda i: (0,) * arr.ndim)
    return pl.pallas_call(
        functools.partial(_select_kernel, tile=tile),
        grid=(t // tile,),
        in_specs=[pl.BlockSpec((tile, D_MODEL), lambda i: (i, 0)), full(pq), full(keys), full(meta)],
        out_specs=[pl.BlockSpec((D_MODEL, tile), lambda i: (0, i))] + [dense_spec] * 4,
        out_shape=[jax.ShapeDtypeStruct((D_MODEL, t), BF16), dense(F32), dense(F32), dense(BF16), dense(BF16)],
        scratch_shapes=[
            pltpu.VMEM((2 * PEER_HEADS, tile, PEER_KEY_DIM), BF16),
            pltpu.VMEM((PEER_TOPK, LANES), F32),
            pltpu.VMEM((PEER_TOPK, LANES), F32),
        ],
        compiler_params=_params(("parallel",)),
        name="peer_select",
    )(x, pq, keys, meta)


def _activations(xt_ref, u_ref, act_ref):
    z = _dot(u_ref[...], xt_ref[...])
    half = z * (1.0 / math.sqrt(2.0))
    act_ref[...] = (half + half * lax.erf(z)).astype(BF16)


def _gate_activations(block, e1_ref, c1_ref, e2_ref, r2_ref, act_ref, w_ref, rows_per_block):
    tile = act_ref.shape[1]
    for jj in range(rows_per_block):
        j = block * rows_per_block + jj
        c1 = [jnp.broadcast_to(c1_ref[h, pl.ds(j, 1), :], (BF16_ROWS, tile)).astype(BF16)
              for h in range(PEER_HEADS)]
        e1 = [jnp.broadcast_to(e1_ref[h, pl.ds(j, 1), :], (BF16_ROWS, tile)).astype(BF16)
              for h in range(PEER_HEADS)]
        for g in range(PEER_NKEYS // BF16_ROWS):
            keys = slice(g * BF16_ROWS, (g + 1) * BF16_ROWS)
            gate = None
            for h in range(PEER_HEADS):
                picked = jnp.where(r2_ref[h, keys, :] < c1[h], e2_ref[h, keys, :], jnp.zeros((), BF16))
                term = picked * e1[h]
                gate = term if gate is None else gate + term
            rows = slice(jj * PEER_NKEYS + g * BF16_ROWS, jj * PEER_NKEYS + (g + 1) * BF16_ROWS)
            w_ref[rows, :] = gate * act_ref[rows, :]


def _dense_kernel(xt_ref, e1_ref, c1_ref, e2_ref, r2_ref, u_ref, vtp_ref, *refs, rows_per_block, nb):
    vt_refs, o_ref = refs[:nb], refs[nb]
    act_scr, w_scr = refs[nb + 1:2 * nb + 1], refs[2 * nb + 1:]
    e = pl.program_id(1)
    nrows = rows_per_block * PEER_NKEYS

    @pl.when(e == 0)
    def _():
        o_ref[...] = jnp.zeros_like(o_ref)
        w_scr[nb - 1][...] = jnp.zeros_like(w_scr[nb - 1])

    def activations(i):
        _activations(xt_ref, u_ref.at[i * nrows:(i + 1) * nrows, :], act_scr[i])

    activations(0)
    o_ref[...] += _dot(vtp_ref[...], w_scr[nb - 1][...])
    for i in range(nb):
        _gate_activations(nb * e + i, e1_ref, c1_ref, e2_ref, r2_ref, act_scr[i], w_scr[i], rows_per_block)
        if i + 1 < nb:
            activations(i + 1)
            o_ref[...] += _dot(vt_refs[i][...], w_scr[i][...])

    @pl.when(e == pl.num_programs(1) - 1)
    def _():
        o_ref[...] += _dot(vt_refs[nb - 1][...], w_scr[nb - 1][...])


def _peer_dense(xt, e1, c1, e2, r2, u, vt, tile, eblock, nb):
    t = xt.shape[1]
    nsteps = PEER_N_EXPERTS // (nb * eblock)
    row_spec = pl.BlockSpec((PEER_HEADS, PEER_NKEYS, tile), lambda i, e: (0, 0, i))
    vt_spec = lambda fn: pl.BlockSpec((D_MODEL, eblock), fn)

    def vt_block(i, e, k):
        block = nb * e + k
        return (0, jnp.where(e == nsteps - 1, block, 0) if k == nb - 1 else block)

    return pl.pallas_call(
        functools.partial(_dense_kernel, rows_per_block=eblock // PEER_NKEYS, nb=nb),
        grid=(t // tile, nsteps),
        in_specs=[pl.BlockSpec((D_MODEL, tile), lambda i, e: (0, i))] + [row_spec] * 4 + [
            pl.BlockSpec((nb * eblock, D_MODEL), lambda i, e: (e, 0)),
            vt_spec(lambda i, e: (0, jnp.maximum(nb * e - 1, 0))),
        ] + [vt_spec(functools.partial(vt_block, k=k)) for k in range(nb)],
        out_specs=pl.BlockSpec((D_MODEL, tile), lambda i, e: (0, i)),
        out_shape=jax.ShapeDtypeStruct((D_MODEL, t), F32),
        scratch_shapes=[pltpu.VMEM((eblock, tile), BF16)] * (2 * nb),
        compiler_params=_params(("parallel", "arbitrary")),
        name="peer_dense",
    )(xt, e1, c1, e2, r2, u, vt, *([vt] * nb))


def _final_kernel(x_ref, cmt_ref, p_ref, wg_ref, wp_ref, g_ref, b_ref, o_ref):
    x = x_ref[...]
    gate = jax.nn.sigmoid(_dot(x.astype(BF16), wg_ref[...]))
    ple = gate * _dot(p_ref[...].astype(BF16), wp_ref[...])
    o_ref[...] = _layer_norm(DEEPNORM_ALPHA * x + cmt_ref[...].T + ple, g_ref[...], b_ref[...])


def _final(x, cmt, p_all, layer, wg, wp, g, b, tile):
    t = x.shape[0]
    full = lambda arr: pl.BlockSpec(arr.shape, lambda i: (0,) * arr.ndim)
    return pl.pallas_call(
        _final_kernel,
        grid=(t // tile,),
        in_specs=[pl.BlockSpec((tile, D_MODEL), lambda i: (i, 0)),
                  pl.BlockSpec((D_MODEL, tile), lambda i: (0, i)),
                  pl.BlockSpec((None, tile, PLE_DIM), lambda i: (layer, i, 0)),
                  full(wg), full(wp), full(g), full(b)],
        out_specs=pl.BlockSpec((tile, D_MODEL), lambda i: (i, 0)),
        out_shape=jax.ShapeDtypeStruct((t, D_MODEL), F32),
        compiler_params=_params(("parallel",)),
        name="ple_ln2",
    )(x, cmt, p_all, wg, wp, g, b)


def _channel_dft():
    c = np.arange(FOURIER_GROUP_DIM)
    ang = 2.0 * np.pi * ((c[:, None] * c[None, :]) % FOURIER_GROUP_DIM) / FOURIER_GROUP_DIM
    eye = np.eye(N_FOURIER_GROUPS)
    return np.concatenate([np.kron(eye, np.cos(ang)), np.kron(eye, -np.sin(ang))], axis=1)


def _sequence_dft(s, scale):
    s_lo = 64
    s_hi = s // s_lo
    k = np.arange(s)
    ang_hi = 2.0 * np.pi * ((k[:, None] * np.arange(s_hi)[None, :]) % s_hi) / s_hi
    ang_lo = 2.0 * np.pi * ((k[:, None] * np.arange(s_lo)[None, :]) % s) / s
    ch = jnp.asarray(np.cos(ang_hi) * scale, F32)[:, :, None]
    sh = jnp.asarray(np.sin(ang_hi) * scale, F32)[:, :, None]
    cl = jnp.asarray(np.cos(ang_lo), F32)[:, None, :]
    sl = jnp.asarray(np.sin(ang_lo), F32)[:, None, :]
    cmat = (ch * cl - sh * sl).reshape(s, s).astype(BF16)
    smat = (sh * cl + ch * sl).reshape(s, s).astype(BF16)
    return cmat, smat


def _attn_bias():
    qi = np.arange(ATTN_BLOCK)
    kj = np.arange(3 * ATTN_BLOCK)
    rel = np.abs(qi[:, None] + ATTN_BLOCK - kj[None, :]).astype(np.float64)
    slopes = np.exp2(-8.0 * np.arange(1, N_HEADS + 1) / N_HEADS)
    bias = np.where(rel[None] <= WINDOW, -slopes[:, None, None] * rel[None], NEG_INF)
    return bias.astype(np.float32)


def _pick_tile(n, pref):
    tile = min(n, pref)
    assert n % tile == 0, (n, tile)
    return tile


def _tiles(t, s):
    return dict(
        rows=_pick_tile(t, 512),
        dft_rows=_pick_tile(s, 256),
        select_tokens=_pick_tile(t, 4 * LANES),
        dense_tokens=_pick_tile(t, 512),
        dense_experts=512,
        dense_blocks=2,
        attn_qblocks=_pick_tile(s // ATTN_BLOCK, 8),
    )


def _trunk(x, p_all, consts, emb_ln, layers):
    bsz, s, _ = x.shape
    t = bsz * s
    mch, cmat, smat, bias = consts
    tiles = _tiles(t, s)
    x = x.reshape(t, D_MODEL)
    p_all = p_all.reshape(p_all.shape[0], t, PLE_DIM)
    for i, lw in enumerate(layers):
        if i == 0:
            x, ab, q, k, v, gt = _inproj(x, emb_ln, lw["w_in"], mch, tiles["rows"])
        else:
            ab, q, k, v, gt = _inproj(x, None, lw["w_in"], mch, tiles["rows"])
        y = _seqdft(ab.reshape(bsz, s, 2 * FOURIER_WIDTH), cmat, smat, tiles["dft_rows"])
        a = _attention(q.reshape(bsz, s, ATTN_WIDTH), k.reshape(bsz, s, KV_WIDTH),
                       v.reshape(bsz, s, KV_WIDTH), bias, lw["sink"], tiles["attn_qblocks"])
        x = _merge(x, y.reshape(t, FOURIER_WIDTH), a.reshape(t, ATTN_WIDTH), gt,
                   lw["w_fo"], lw["w_ao"], lw["w_out"], lw["ln1_g"], lw["ln1_b"], tiles["rows"])
        xt, e1, c1, e2, r2 = _peer_select(x, lw["pq"], lw["keys"], tiles["select_tokens"])
        cmt = _peer_dense(xt, e1, c1, e2, r2, lw["u"], lw["vt"], tiles["dense_tokens"],
                          tiles["dense_experts"], tiles["dense_blocks"])
        x = _final(x, cmt, p_all, i, lw["wg"], lw["wp"], lw["ln2_g"], lw["ln2_b"], tiles["rows"])
    return x.reshape(bsz, s, D_MODEL)


def _prepare_layer(i, w_in, attn_sink, w_fourier_out, w_attn_out, w_out, ln1_g, ln1_b, peer_w_q,
                   peer_keys, peer_u, peer_v, ple_w_gate, ple_w_proj, ln2_g, ln2_b):
    w = w_in[i]
    wq = w[:, _O_Q:_O_K].reshape(D_MODEL, N_KV_HEADS, Q_GROUP, HEAD_DIM).transpose(0, 2, 1, 3)
    wq = wq.reshape(D_MODEL, ATTN_WIDTH) * (HEAD_DIM ** -0.5)
    w_perm = jnp.concatenate([w[:, :_O_Q], wq, w[:, _O_K:]], axis=1).astype(BF16)
    wao = w_attn_out[i].reshape(N_KV_HEADS, Q_GROUP, HEAD_DIM, D_MODEL).transpose(1, 0, 2, 3)
    row = lambda a: a[i].reshape(1, D_MODEL).astype(F32)
    return dict(
        w_in=w_perm, sink=attn_sink[i].astype(F32),
        w_fo=w_fourier_out[i].astype(BF16), w_ao=wao.reshape(ATTN_WIDTH, D_MODEL).astype(BF16),
        w_out=w_out[i].astype(BF16), ln1_g=row(ln1_g), ln1_b=row(ln1_b),
        pq=peer_w_q[i].astype(BF16), keys=peer_keys[i].astype(BF16),
        u=(peer_u[i] * (1.0 / math.sqrt(2.0))).astype(BF16), vt=peer_v[i].astype(BF16).T,
        wg=ple_w_gate[i].astype(BF16), wp=ple_w_proj[i].astype(BF16), ln2_g=row(ln2_g), ln2_b=row(ln2_b),
    )


def kernel(x_prompt, x_sample, p_prompt, p_sample, emb_ln_g, emb_ln_b, w_in, attn_sink, w_fourier_out, w_attn_out, w_out, ln1_g, ln1_b, peer_w_q, peer_keys, peer_u, peer_v, ple_w_gate, ple_w_proj, ln2_g, ln2_b):
    depth = w_in.shape[0]
    layers = [_prepare_layer(i, w_in, attn_sink, w_fourier_out, w_attn_out, w_out, ln1_g, ln1_b, peer_w_q,
                             peer_keys, peer_u, peer_v, ple_w_gate, ple_w_proj, ln2_g, ln2_b)
              for i in range(depth)]
    emb_ln = (emb_ln_g.reshape(1, D_MODEL).astype(F32), emb_ln_b.reshape(1, D_MODEL).astype(F32))
    mch = jnp.asarray(_channel_dft() / math.sqrt(FOURIER_GROUP_DIM), BF16)
    bias = jnp.asarray(_attn_bias())
    outs = []
    for x, p in ((x_prompt, p_prompt), (x_sample, p_sample)):
        s = x.shape[1]
        cmat, smat = _sequence_dft(s, 1.0 / math.sqrt(s))
        outs.append(_trunk(x, p, (mch, cmat, smat, bias), emb_ln, layers))
    return tuple(outs)
```

```python
import functools
import math

import numpy as np
import jax
import jax.numpy as jnp
from jax import lax
from jax.experimental import pallas as pl
from jax.experimental.pallas import tpu as pltpu

F32 = jnp.float32
BF16 = jnp.bfloat16

D_MODEL = 1024
N_HEADS = 8
N_KV_HEADS = 2
Q_GROUP = N_HEADS // N_KV_HEADS
HEAD_DIM = 64
WINDOW = 128
ATTN_BLOCK = 128
ATTN_WIDTH = N_HEADS * HEAD_DIM
KV_WIDTH = N_KV_HEADS * HEAD_DIM
N_FOURIER_GROUPS = 8
FOURIER_GROUP_DIM = 64
FOURIER_WIDTH = N_FOURIER_GROUPS * FOURIER_GROUP_DIM
GATE_WIDTH = 2 * D_MODEL
IN_WIDTH = FOURIER_WIDTH + ATTN_WIDTH + 2 * KV_WIDTH + GATE_WIDTH
PEER_HEADS = 8
PEER_NKEYS = 128
PEER_N_EXPERTS = PEER_NKEYS * PEER_NKEYS
PEER_TOPK = 16
PEER_KEY_DIM = 128
PLE_DIM = 256
DEPTH = 2
DEEPNORM_ALPHA = (2 * DEPTH) ** 0.25
LN_EPS = 1e-5
GELU_FOLD = 1.0 / math.sqrt(2.0)
NEG_INF = -1e30

LANES = 128
SUBLANES = 8
BF16_ROWS = 16
VMEM_LIMIT = 52 * 1024 * 1024

_O_F = 0
_O_Q = _O_F + FOURIER_WIDTH
_O_K = _O_Q + ATTN_WIDTH
_O_V = _O_K + KV_WIDTH
_O_G = _O_V + KV_WIDTH

_CAND_GROUPS = [(0, 0), (0, 8), (1, 0), (2, 0), (3, 0), (4, 0), (5, 0), (6, 0), (7, 0)]
_CAND_ROWS = 8 * (len(_CAND_GROUPS) + 1)
_BIG_INDEX = 1.0e6


def _params(semantics, flags=None):
    return pltpu.CompilerParams(dimension_semantics=semantics, vmem_limit_bytes=VMEM_LIMIT, flags=flags)


def _layer_norm(x, g, b):
    mu = jnp.mean(x, axis=-1, keepdims=True)
    xc = x - mu
    var = jnp.mean(xc * xc, axis=-1, keepdims=True)
    return xc * lax.rsqrt(var + LN_EPS) * g + b


def _dot(a, b):
    return jnp.dot(a, b, preferred_element_type=F32)


def _dot_nt(a, b):
    return lax.dot_general(a, b, (((1,), (1,)), ((), ())), preferred_element_type=F32)


def _inproj_kernel(*refs, apply_ln):
    if apply_ln:
        x_ref, g_ref, b_ref, w_ref, mch_ref, xn_ref, ab_ref, q_ref, k_ref, v_ref, gt_ref = refs
        x = _layer_norm(x_ref[...], g_ref[...], b_ref[...])
        xn_ref[...] = x
    else:
        x_ref, w_ref, mch_ref, ab_ref, q_ref, k_ref, v_ref, gt_ref = refs
        x = x_ref[...]
    xb = x.astype(BF16)
    f = _dot(xb, w_ref[:, _O_F:_O_Q])
    ab_ref[...] = _dot(f.astype(BF16), mch_ref[...]).astype(BF16)
    q_ref[...] = _dot(xb, w_ref[:, _O_Q:_O_K]).astype(BF16)
    k_ref[...] = _dot(xb, w_ref[:, _O_K:_O_V]).astype(BF16)
    v_ref[...] = _dot(xb, w_ref[:, _O_V:_O_G]).astype(BF16)
    gt_ref[...] = jax.nn.sigmoid(_dot(xb, w_ref[:, _O_G:IN_WIDTH])).astype(BF16)


def _inproj(x, ln, w_in, mch, tile):
    t = x.shape[0]
    apply_ln = ln is not None
    row = lambda width: pl.BlockSpec((tile, width), lambda i: (i, 0))
    full = lambda a: pl.BlockSpec(a.shape, lambda i: (0,) * a.ndim)
    ins = [x] + ([ln[0], ln[1]] if apply_ln else []) + [w_in, mch]
    in_specs = [row(D_MODEL)] + ([full(ln[0]), full(ln[1])] if apply_ln else []) + [full(w_in), full(mch)]
    widths = [2 * FOURIER_WIDTH, ATTN_WIDTH, KV_WIDTH, KV_WIDTH, GATE_WIDTH]
    out_shape = [jax.ShapeDtypeStruct((t, w), BF16) for w in widths]
    out_specs = [row(w) for w in widths]
    if apply_ln:
        out_shape = [jax.ShapeDtypeStruct((t, D_MODEL), F32)] + out_shape
        out_specs = [row(D_MODEL)] + out_specs
    return pl.pallas_call(
        functools.partial(_inproj_kernel, apply_ln=apply_ln),
        grid=(t // tile,),
        in_specs=in_specs,
        out_specs=out_specs,
        out_shape=out_shape,
        compiler_params=_params(("parallel",)),
        name="inproj_ln" if apply_ln else "inproj",
    )(*ins)


def _seqdft_kernel(c_ref, s_ref, a_ref, b_ref, y_ref):
    y_ref[...] = (_dot(c_ref[...], a_ref[...]) + _dot(s_ref[...], b_ref[...])).astype(BF16)


def _seqdft(ab, cmat, smat, tile):
    bsz, s, _ = ab.shape
    return pl.pallas_call(
        _seqdft_kernel,
        grid=(bsz, s // tile),
        in_specs=[
            pl.BlockSpec((tile, s), lambda b, m: (m, 0)),
            pl.BlockSpec((tile, s), lambda b, m: (m, 0)),
            pl.BlockSpec((None, s, FOURIER_WIDTH), lambda b, m: (b, 0, 0)),
            pl.BlockSpec((None, s, FOURIER_WIDTH), lambda b, m: (b, 0, 1)),
        ],
        out_specs=pl.BlockSpec((None, tile, FOURIER_WIDTH), lambda b, m: (b, m, 0)),
        out_shape=jax.ShapeDtypeStruct((bsz, s, FOURIER_WIDTH), BF16),
        compiler_params=_params(("parallel", "parallel")),
        name="seqdft",
    )(cmat, smat, ab, ab)


def _attn_kernel(sink_ref, q_ref, kp_ref, kc_ref, kn_ref, vp_ref, vc_ref, vn_ref, bias_ref, o_ref, *, nsteps, qblocks):
    n = pl.program_id(1)
    col = lax.broadcasted_iota(jnp.int32, (1, 3 * ATTN_BLOCK), 1)
    kall = jnp.concatenate([kp_ref[...], kc_ref[...], kn_ref[...]], axis=0)
    vall = jnp.concatenate([vp_ref[...], vc_ref[...], vn_ref[...]], axis=0)
    vones = jnp.concatenate([vall, jnp.ones_like(vall)], axis=1)
    lane = lax.broadcasted_iota(jnp.int32, (ATTN_BLOCK, KV_WIDTH), 1)
    low = lane < HEAD_DIM
    for j in range(qblocks):
        rows = slice(j * ATTN_BLOCK, (j + 1) * ATTN_BLOCK)
        window = slice(j * ATTN_BLOCK, (j + 3) * ATTN_BLOCK)
        off_edge = None
        if j == 0:
            off_edge = (col < ATTN_BLOCK) & (n == 0)
        if j == qblocks - 1:
            after = (col >= 2 * ATTN_BLOCK) & (n == nsteps - 1)
            off_edge = after if off_edge is None else off_edge | after
        kcat = kall[window]
        vcat = vones[window]
        for g in range(Q_GROUP):
            qg = q_ref[rows, g * KV_WIDTH:(g + 1) * KV_WIDTH]
            halves = []
            for kh in range(N_KV_HEADS):
                h = kh * Q_GROUP + g
                qm = jnp.where(low if kh == 0 else jnp.logical_not(low), qg, jnp.zeros_like(qg))
                s = _dot_nt(qm, kcat) + bias_ref[h]
                if off_edge is not None:
                    s = s + jnp.where(off_edge, NEG_INF, 0.0).astype(F32)
                sink = sink_ref[h]
                m = jnp.maximum(jnp.max(s, axis=-1, keepdims=True), sink)
                p = jnp.exp(s - m)
                pv = _dot(p.astype(BF16), vcat)
                halves.append(pv[:, :KV_WIDTH] / (pv[:, KV_WIDTH:] + jnp.exp(sink - m)))
            o_ref[rows, g * KV_WIDTH:(g + 1) * KV_WIDTH] = jnp.where(low, halves[0], halves[1]).astype(BF16)


def _attention(q, k, v, bias, sink, qblocks):
    bsz, s, _ = q.shape
    nb = s // ATTN_BLOCK
    nsteps = nb // qblocks
    edge_spec = lambda fn: pl.BlockSpec((None, ATTN_BLOCK, KV_WIDTH), fn)
    prev = lambda b, n: (b, jnp.maximum(qblocks * n - 1, 0), 0)
    cur = lambda b, n: (b, n, 0)
    nxt = lambda b, n: (b, jnp.minimum(qblocks * (n + 1), nb - 1), 0)
    mid_spec = pl.BlockSpec((None, qblocks * ATTN_BLOCK, KV_WIDTH), cur)
    return pl.pallas_call(
        functools.partial(_attn_kernel, nsteps=nsteps, qblocks=qblocks),
        grid=(bsz, nsteps),
        in_specs=[
            pl.BlockSpec(memory_space=pltpu.SMEM),
            pl.BlockSpec((None, qblocks * ATTN_BLOCK, ATTN_WIDTH), cur),
            edge_spec(prev), mid_spec, edge_spec(nxt),
            edge_spec(prev), mid_spec, edge_spec(nxt),
            pl.BlockSpec(bias.shape, lambda b, n: (0, 0, 0)),
        ],
        out_specs=pl.BlockSpec((None, qblocks * ATTN_BLOCK, ATTN_WIDTH), cur),
        out_shape=jax.ShapeDtypeStruct((bsz, s, ATTN_WIDTH), BF16),
        compiler_params=_params(("parallel", "parallel")),
        name="window_attn",
    )(sink, q, k, k, k, v, v, v, bias)


def _merge_kernel(x_ref, y_ref, a_ref, gt_ref, wfo_ref, wao_ref, wout_ref, g_ref, b_ref, o_ref):
    f = _dot(y_ref[...], wfo_ref[...])
    a = _dot(a_ref[...], wao_ref[...])
    merged = gt_ref[:, :D_MODEL].astype(F32) * f + gt_ref[:, D_MODEL:].astype(F32) * a
    o = _dot(merged.astype(BF16), wout_ref[...])
    o_ref[...] = _layer_norm(DEEPNORM_ALPHA * x_ref[...] + o, g_ref[...], b_ref[...])


def _merge(x, y, a, gt, wfo, wao, wout, g, b, tile):
    t = x.shape[0]
    row = lambda width: pl.BlockSpec((tile, width), lambda i: (i, 0))
    full = lambda arr: pl.BlockSpec(arr.shape, lambda i: (0,) * arr.ndim)
    return pl.pallas_call(
        _merge_kernel,
        grid=(t // tile,),
        in_specs=[row(D_MODEL), row(FOURIER_WIDTH), row(ATTN_WIDTH), row(GATE_WIDTH),
                  full(wfo), full(wao), full(wout), full(g), full(b)],
        out_specs=row(D_MODEL),
        out_shape=jax.ShapeDtypeStruct((t, D_MODEL), F32),
        compiler_params=_params(("parallel",)),
        name="merge_ln1",
    )(x, y, a, gt, wfo, wao, wout, g, b)


def _extract_sorted(s, ids, count, vals_ref):
    def body(r, carry):
        s, rank = carry
        m = jnp.max(s, axis=0, keepdims=True)
        first = jnp.min(jnp.where(s == m, ids, _BIG_INDEX), axis=0, keepdims=True)
        sel = ids == first
        if vals_ref is not None:
            vals_ref[pl.ds(r, 1), :] = m
        rank = jnp.where(sel, lax.convert_element_type(r, F32), rank)
        s = jnp.where(sel, -jnp.inf, s)
        return s, rank

    rank0 = jnp.full(s.shape, float(count), F32)
    s, rank = lax.fori_loop(0, count, body, (s, rank0))
    return rank, s


def _select_exact(s1, s2, key_ids, cand_ids, cand_mask, v1_scr, v2_scr):
    rank1, _ = _extract_sorted(s1, key_ids, PEER_TOPK, v1_scr)
    rank2, _ = _extract_sorted(s2, key_ids, PEER_TOPK, v2_scr)
    v1 = v1_scr[...]
    v2 = v2_scr[...]
    groups = [v1[r1:r1 + 1] + v2[lo:lo + SUBLANES] for r1, lo in _CAND_GROUPS]
    groups.append(v1[SUBLANES:] + v2[0:1])
    cand = jnp.concatenate(groups, axis=0) + cand_mask
    crank, _ = _extract_sorted(cand, cand_ids, PEER_TOPK, None)
    chosen = crank < float(PEER_TOPK)
    pexp = jnp.where(chosen, jnp.exp(cand - cand[0:1]), 0.0)
    inv_z = GELU_FOLD / jnp.sum(pexp, axis=0, keepdims=True)
    cnt = chosen.astype(F32)
    counts = [jnp.sum(cnt[0:2 * SUBLANES], axis=0, keepdims=True)]
    for gi in range(2, len(_CAND_GROUPS)):
        counts.append(jnp.sum(cnt[gi * SUBLANES:(gi + 1) * SUBLANES], axis=0, keepdims=True))
    tail = cnt[len(_CAND_GROUPS) * SUBLANES:]
    c1 = jnp.zeros_like(s1)
    for r in range(PEER_TOPK):
        cr = counts[r] if r < SUBLANES else tail[r - SUBLANES:r - SUBLANES + 1]
        c1 = jnp.where(rank1 == float(r), cr, c1)
    e1 = jnp.exp(s1 - v1[0:1]) * inv_z
    e2 = jnp.exp(s2 - v2[0:1])
    return e1, c1, e2, rank2


def _merge_exchange_pairs(n):
    pairs = []
    t = max(1, math.ceil(math.log2(n)))
    p = 1 << (t - 1)
    while p > 0:
        q, r, d = 1 << (t - 1), 0, p
        while d > 0:
            pairs.extend((i, i + d) for i in range(n - d) if (i & p) == r)
            d, q, r = q - p, q >> 1, p
        p >>= 1
    return pairs


def _compare_exchange(v, i, j):
    v[i], v[j] = jnp.maximum(v[i], v[j]), jnp.minimum(v[i], v[j])


def _sort_desc(v):
    v = list(v)
    for i, j in _merge_exchange_pairs(len(v)):
        _compare_exchange(v, i, j)
    return v


def _bitonic_merge(v):
    v = list(v)
    d = len(v) // 2
    while d:
        for k in range(len(v)):
            if not k & d:
                _compare_exchange(v, k, k + d)
        d //= 2
    return v


def _across_sublanes(x, op):
    for shift in (4, 2, 1):
        x = op(x, pltpu.roll(x, shift, 0))
    return x


def _top16_of_sublane_lists(lists):
    for shift in (4, 2, 1):
        other = [pltpu.roll(x, shift, 0) for x in lists]
        n = len(lists)
        merged = []
        for k in range(PEER_TOPK):
            mine = lists[k] if k < n else None
            theirs = other[PEER_TOPK - 1 - k] if PEER_TOPK - 1 - k < n else None
            merged.append(mine if theirs is None else theirs if mine is None else jnp.maximum(mine, theirs))
        lists = _bitonic_merge(merged)
    return lists


def _select_sorted(s1, s2, cand_mask):
    nv = PEER_NKEYS // SUBLANES
    a1 = [s1[j * SUBLANES:(j + 1) * SUBLANES] for j in range(nv)]
    a2 = [s2[j * SUBLANES:(j + 1) * SUBLANES] for j in range(nv)]
    v1 = _top16_of_sublane_lists(_sort_desc(a1))
    v2 = _top16_of_sublane_lists(_sort_desc(a2))
    sub = lax.broadcasted_iota(jnp.int32, (SUBLANES, LANES), 0)

    def by_sublane(vals):
        out = vals[SUBLANES - 1]
        for j in range(SUBLANES - 2, -1, -1):
            out = jnp.where(sub == j, vals[j], out)
        return out

    v2_nat = {0: by_sublane(v2[:SUBLANES]), SUBLANES: by_sublane(v2[SUBLANES:])}
    groups = [v1[r1] + v2_nat[lo] for r1, lo in _CAND_GROUPS]
    groups.append(by_sublane(v1[SUBLANES:]) + v2[0])
    groups = [g + cand_mask[i * SUBLANES:(i + 1) * SUBLANES] for i, g in enumerate(groups)]
    top = _top16_of_sublane_lists(_sort_desc(groups))
    thr = top[PEER_TOPK - 1]
    z = None
    for k in range(PEER_TOPK):
        term = jnp.exp(top[k] - top[0])
        z = term if z is None else z + term
    inv_z = GELU_FOLD / z
    picked = [jnp.where(g >= thr, 1.0, 0.0) for g in groups[:-1]]
    add = lambda a, b: a + b
    counts = [_across_sublanes(picked[0] + picked[1], add)]
    counts += [_across_sublanes(picked[r + 1], add) for r in range(1, SUBLANES)]
    counts += [jnp.where(v1[r] + v2[0] >= thr, 1.0, 0.0) for r in range(SUBLANES, PEER_TOPK)]
    e1, c1, e2, r2 = [], [], [], []
    c_total = None
    r_total = None
    for j in range(nv):
        c = jnp.zeros((SUBLANES, LANES), F32)
        r = jnp.full((SUBLANES, LANES), float(PEER_TOPK), F32)
        for k in range(PEER_TOPK - 1, -1, -1):
            c = jnp.where(a1[j] >= v1[k], counts[k], c)
            r = jnp.where(a2[j] >= v2[k], float(k), r)
        c1.append(c)
        r2.append(r)
        e1.append(jnp.exp(a1[j] - v1[0]) * inv_z)
        e2.append(jnp.exp(a2[j] - v2[0]))
        c_total = c if c_total is None else c_total + c
        r_total = (float(PEER_TOPK) - r) if r_total is None else r_total + (float(PEER_TOPK) - r)
    c_total = _across_sublanes(c_total, add)
    r_total = _across_sublanes(r_total, add)
    distinct_total = float(PEER_TOPK * (PEER_TOPK + 1) // 2)
    bad = jnp.where((c_total != float(PEER_TOPK)) | (r_total != distinct_total), 1.0, 0.0)
    cat = lambda parts: jnp.concatenate(parts, axis=0)
    return cat(e1), cat(c1), cat(e2), cat(r2), bad


def _select_kernel(x_ref, pq_ref, keys_ref, meta_ref, xt_ref, e1_ref, c1_ref, e2_ref, r2_ref,
                   q_scr, v1_scr, v2_scr, *, tile):
    x = x_ref[...]
    xt_ref[...] = x.T.astype(BF16)
    q = _dot(x.astype(BF16), pq_ref[...])
    for hp in range(2 * PEER_HEADS):
        q_scr[hp] = q[:, hp * PEER_KEY_DIM:(hp + 1) * PEER_KEY_DIM].astype(BF16)

    chunks = [slice(c * LANES, (c + 1) * LANES) for c in range(tile // LANES)]

    def scores(h):
        return tuple(_dot_nt(keys_ref[h, half], q_scr[2 * h + half, tok, :]) for tok in chunks for half in (0, 1))

    def head_body(h, s_all):
        s_next = scores(jnp.minimum(h + 1, PEER_HEADS - 1))

        def emit(tok, e1, c1, e2, r2):
            e1_ref[h, :, tok] = e1
            c1_ref[h, :, tok] = c1
            e2_ref[h, :, tok] = e2.astype(BF16)
            r2_ref[h, :, tok] = r2.astype(BF16)

        flags = []
        for c, tok in enumerate(chunks):
            e1, c1, e2, r2, bad = _select_sorted(s_all[2 * c], s_all[2 * c + 1], meta_ref[1])
            emit(tok, e1, c1, e2, r2)
            flags.append(jnp.max(bad))
        for c, tok in enumerate(chunks):
            @pl.when(flags[c] > 0.0)
            def _():
                key_ids = lax.broadcasted_iota(jnp.int32, (PEER_NKEYS, LANES), 0).astype(F32)
                emit(tok, *_select_exact(s_all[2 * c], s_all[2 * c + 1], key_ids, meta_ref[0], meta_ref[1],
                                         v1_scr, v2_scr))
        return s_next

    lax.fori_loop(0, PEER_HEADS, head_body, scores(0))


def _cand_meta():
    ids = np.full((_CAND_ROWS, LANES), _BIG_INDEX, np.float32)
    mask = np.full((_CAND_ROWS, LANES), -np.inf, np.float32)
    for gi, (r1, lo) in enumerate(_CAND_GROUPS):
        for j in range(SUBLANES):
            r2 = lo + j
            if (r1 + 1) * (r2 + 1) <= PEER_TOPK:
                ids[gi * SUBLANES + j] = r1 * PEER_TOPK + r2
                mask[gi * SUBLANES + j] = 0.0
    base = len(_CAND_GROUPS) * SUBLANES
    for j in range(SUBLANES):
        ids[base + j] = (SUBLANES + j) * PEER_TOPK
        mask[base + j] = 0.0
    return np.stack([ids, mask])


def _peer_select(x, pq, keys, tile):
    t = x.shape[0]
    meta = jnp.asarray(_cand_meta())
    dense = lambda dtype: jax.ShapeDtypeStruct((PEER_HEADS, PEER_NKEYS, t), dtype)
    dense_spec = pl.BlockSpec((PEER_HEADS, PEER_NKEYS, tile), lambda i: (0, 0, i))
    full = lambda arr: pl.BlockSpec(arr.shape, lambda i: (0,) * arr.ndim)
    return pl.pallas_call(
        functools.partial(_select_kernel, tile=tile),
        grid=(t // tile,),
        in_specs=[pl.BlockSpec((tile, D_MODEL), lambda i: (i, 0)), full(pq), full(keys), full(meta)],
        out_specs=[pl.BlockSpec((D_MODEL, tile), lambda i: (0, i))] + [dense_spec] * 4,
        out_shape=[jax.ShapeDtypeStruct((D_MODEL, t), BF16), dense(F32), dense(F32), dense(BF16), dense(BF16)],
        scratch_shapes=[
            pltpu.VMEM((2 * PEER_HEADS, tile, PEER_KEY_DIM), BF16),
            pltpu.VMEM((PEER_TOPK, LANES), F32),
            pltpu.VMEM((PEER_TOPK, LANES), F32),
        ],
        compiler_params=_params(("parallel",)),
        name="peer_select",
    )(x, pq, keys, meta)


def _activations(xt_ref, u_ref, act_ref):
    z = _dot(u_ref[...], xt_ref[...])
    act_ref[...] = (z + z * lax.erf(z)).astype(BF16)


def _gate_activations(block, e1_ref, c1_ref, e2_ref, r2_ref, act_ref, w_ref, rows_per_block):
    tile = act_ref.shape[1]
    for jj in range(rows_per_block):
        j = block * rows_per_block + jj
        c1 = [jnp.broadcast_to(c1_ref[h, pl.ds(j, 1), :], (BF16_ROWS, tile)).astype(BF16)
              for h in range(PEER_HEADS)]
        e1 = [jnp.broadcast_to(e1_ref[h, pl.ds(j, 1), :], (BF16_ROWS, tile)).astype(BF16)
              for h in range(PEER_HEADS)]
        for g in range(PEER_NKEYS // BF16_ROWS):
            keys = slice(g * BF16_ROWS, (g + 1) * BF16_ROWS)
            gate = None
            for h in range(PEER_HEADS):
                picked = jnp.where(r2_ref[h, keys, :] < c1[h], e2_ref[h, keys, :], jnp.zeros((), BF16))
                term = picked * e1[h]
                gate = term if gate is None else gate + term
            rows = slice(jj * PEER_NKEYS + g * BF16_ROWS, jj * PEER_NKEYS + (g + 1) * BF16_ROWS)
            w_ref[rows, :] = gate * act_ref[rows, :]


def _dense_kernel(xt_ref, e1_ref, c1_ref, e2_ref, r2_ref, u_ref, vtp_ref, *refs, rows_per_block, nb):
    vt_refs, o_ref = refs[:nb], refs[nb]
    act_scr, w_scr = refs[nb + 1:2 * nb + 1], refs[2 * nb + 1:]
    e = pl.program_id(1)
    nrows = rows_per_block * PEER_NKEYS

    @pl.when(e == 0)
    def _():
        o_ref[...] = jnp.zeros_like(o_ref)
        w_scr[nb - 1][...] = jnp.zeros_like(w_scr[nb - 1])

    def activations(i):
        _activations(xt_ref, u_ref.at[i * nrows:(i + 1) * nrows, :], act_scr[i])

    activations(0)
    o_ref[...] += _dot(vtp_ref[...], w_scr[nb - 1][...])
    for i in range(nb):
        _gate_activations(nb * e + i, e1_ref, c1_ref, e2_ref, r2_ref, act_scr[i], w_scr[i], rows_per_block)
        if i + 1 < nb:
            activations(i + 1)
            o_ref[...] += _dot(vt_refs[i][...], w_scr[i][...])

    @pl.when(e == pl.num_programs(1) - 1)
    def _():
        o_ref[...] += _dot(vt_refs[nb - 1][...], w_scr[nb - 1][...])


def _peer_dense(xt, e1, c1, e2, r2, u, vt, tile, eblock, nb):
    t = xt.shape[1]
    nsteps = PEER_N_EXPERTS // (nb * eblock)
    row_spec = pl.BlockSpec((PEER_HEADS, PEER_NKEYS, tile), lambda i, e: (0, 0, i))
    vt_spec = lambda fn: pl.BlockSpec((D_MODEL, eblock), fn)

    def vt_block(i, e, k):
        block = nb * e + k
        return (0, jnp.where(e == nsteps - 1, block, 0) if k == nb - 1 else block)

    return pl.pallas_call(
        functools.partial(_dense_kernel, rows_per_block=eblock // PEER_NKEYS, nb=nb),
        grid=(t // tile, nsteps),
        in_specs=[pl.BlockSpec((D_MODEL, tile), lambda i, e: (0, i))] + [row_spec] * 4 + [
            pl.BlockSpec((nb * eblock, D_MODEL), lambda i, e: (e, 0)),
            vt_spec(lambda i, e: (0, jnp.maximum(nb * e - 1, 0))),
        ] + [vt_spec(functools.partial(vt_block, k=k)) for k in range(nb)],
        out_specs=pl.BlockSpec((D_MODEL, tile), lambda i, e: (0, i)),
        out_shape=jax.ShapeDtypeStruct((D_MODEL, t), F32),
        scratch_shapes=[pltpu.VMEM((eblock, tile), BF16)] * (2 * nb),
        compiler_params=_params(("parallel", "arbitrary")),
        name="peer_dense",
    )(xt, e1, c1, e2, r2, u, vt, *([vt] * nb))


def _final_kernel(x_ref, cmt_ref, p_ref, wg_ref, wp_ref, g_ref, b_ref, o_ref):
    x = x_ref[...]
    gate = jax.nn.sigmoid(_dot(x.astype(BF16), wg_ref[...]))
    ple = gate * _dot(p_ref[...].astype(BF16), wp_ref[...])
    o_ref[...] = _layer_norm(DEEPNORM_ALPHA * x + cmt_ref[...].T + ple, g_ref[...], b_ref[...])


def _final(x, cmt, p_all, layer, wg, wp, g, b, tile):
    t = x.shape[0]
    full = lambda arr: pl.BlockSpec(arr.shape, lambda i: (0,) * arr.ndim)
    return pl.pallas_call(
        _final_kernel,
        grid=(t // tile,),
        in_specs=[pl.BlockSpec((tile, D_MODEL), lambda i: (i, 0)),
                  pl.BlockSpec((D_MODEL, tile), lambda i: (0, i)),
                  pl.BlockSpec((None, tile, PLE_DIM), lambda i: (layer, i, 0)),
                  full(wg), full(wp), full(g), full(b)],
        out_specs=pl.BlockSpec((tile, D_MODEL), lambda i: (i, 0)),
        out_shape=jax.ShapeDtypeStruct((t, D_MODEL), F32),
        compiler_params=_params(("parallel",)),
        name="ple_ln2",
    )(x, cmt, p_all, wg, wp, g, b)


def _channel_dft():
    c = np.arange(FOURIER_GROUP_DIM)
    ang = 2.0 * np.pi * ((c[:, None] * c[None, :]) % FOURIER_GROUP_DIM) / FOURIER_GROUP_DIM
    eye = np.eye(N_FOURIER_GROUPS)
    return np.concatenate([np.kron(eye, np.cos(ang)), np.kron(eye, -np.sin(ang))], axis=1)


def _sequence_dft(s, scale):
    s_lo = 64
    s_hi = s // s_lo
    k = np.arange(s)
    ang_hi = 2.0 * np.pi * ((k[:, None] * np.arange(s_hi)[None, :]) % s_hi) / s_hi
    ang_lo = 2.0 * np.pi * ((k[:, None] * np.arange(s_lo)[None, :]) % s) / s
    ch = jnp.asarray(np.cos(ang_hi) * scale, F32)[:, :, None]
    sh = jnp.asarray(np.sin(ang_hi) * scale, F32)[:, :, None]
    cl = jnp.asarray(np.cos(ang_lo), F32)[:, None, :]
    sl = jnp.asarray(np.sin(ang_lo), F32)[:, None, :]
    cmat = (ch * cl - sh * sl).reshape(s, s).astype(BF16)
    smat = (sh * cl + ch * sl).reshape(s, s).astype(BF16)
    return cmat, smat


def _attn_bias():
    qi = np.arange(ATTN_BLOCK)
    kj = np.arange(3 * ATTN_BLOCK)
    rel = np.abs(qi[:, None] + ATTN_BLOCK - kj[None, :]).astype(np.float64)
    slopes = np.exp2(-8.0 * np.arange(1, N_HEADS + 1) / N_HEADS)
    bias = np.where(rel[None] <= WINDOW, -slopes[:, None, None] * rel[None], NEG_INF)
    return bias.astype(np.float32)


def _pick_tile(n, pref):
    tile = min(n, pref)
    assert n % tile == 0, (n, tile)
    return tile


def _tiles(t, s):
    return dict(
        rows=_pick_tile(t, 512),
        dft_rows=_pick_tile(s, 512),
        select_tokens=_pick_tile(t, 4 * LANES),
        dense_tokens=_pick_tile(t, 512),
        dense_experts=512,
        dense_blocks=2,
        attn_qblocks=_pick_tile(s // ATTN_BLOCK, 8),
    )


def _trunk(x, p_all, consts, emb_ln, layers):
    bsz, s, _ = x.shape
    t = bsz * s
    mch, cmat, smat, bias = consts
    tiles = _tiles(t, s)
    x = x.reshape(t, D_MODEL)
    p_all = p_all.reshape(p_all.shape[0], t, PLE_DIM)
    for i, lw in enumerate(layers):
        if i == 0:
            x, ab, q, k, v, gt = _inproj(x, emb_ln, lw["w_in"], mch, tiles["rows"])
        else:
            ab, q, k, v, gt = _inproj(x, None, lw["w_in"], mch, tiles["rows"])
        y = _seqdft(ab.reshape(bsz, s, 2 * FOURIER_WIDTH), cmat, smat, tiles["dft_rows"])
        a = _attention(q.reshape(bsz, s, ATTN_WIDTH), k.reshape(bsz, s, KV_WIDTH),
                       v.reshape(bsz, s, KV_WIDTH), bias, lw["sink"], tiles["attn_qblocks"])
        x = _merge(x, y.reshape(t, FOURIER_WIDTH), a.reshape(t, ATTN_WIDTH), gt,
                   lw["w_fo"], lw["w_ao"], lw["w_out"], lw["ln1_g"], lw["ln1_b"], tiles["rows"])
        xt, e1, c1, e2, r2 = _peer_select(x, lw["pq"], lw["keys"], tiles["select_tokens"])
        cmt = _peer_dense(xt, e1, c1, e2, r2, lw["u"], lw["vt"], tiles["dense_tokens"],
                          tiles["dense_experts"], tiles["dense_blocks"])
        x = _final(x, cmt, p_all, i, lw["wg"], lw["wp"], lw["ln2_g"], lw["ln2_b"], tiles["rows"])
    return x.reshape(bsz, s, D_MODEL)


def _prepare_layer(i, w_in, attn_sink, w_fourier_out, w_attn_out, w_out, ln1_g, ln1_b, peer_w_q,
                   peer_keys, peer_u, peer_v, ple_w_gate, ple_w_proj, ln2_g, ln2_b):
    w = w_in[i]
    wq = w[:, _O_Q:_O_K].reshape(D_MODEL, N_KV_HEADS, Q_GROUP, HEAD_DIM).transpose(0, 2, 1, 3)
    wq = wq.reshape(D_MODEL, ATTN_WIDTH) * (HEAD_DIM ** -0.5)
    w_perm = jnp.concatenate([w[:, :_O_Q], wq, w[:, _O_K:]], axis=1).astype(BF16)
    wao = w_attn_out[i].reshape(N_KV_HEADS, Q_GROUP, HEAD_DIM, D_MODEL).transpose(1, 0, 2, 3)
    row = lambda a: a[i].reshape(1, D_MODEL).astype(F32)
    return dict(
        w_in=w_perm, sink=attn_sink[i].astype(F32),
        w_fo=w_fourier_out[i].astype(BF16), w_ao=wao.reshape(ATTN_WIDTH, D_MODEL).astype(BF16),
        w_out=w_out[i].astype(BF16), ln1_g=row(ln1_g), ln1_b=row(ln1_b),
        pq=peer_w_q[i].astype(BF16), keys=peer_keys[i].astype(BF16),
        u=(peer_u[i] * (1.0 / math.sqrt(2.0))).astype(BF16), vt=peer_v[i].astype(BF16).T,
        wg=ple_w_gate[i].astype(BF16), wp=ple_w_proj[i].astype(BF16), ln2_g=row(ln2_g), ln2_b=row(ln2_b),
    )


def kernel(x_prompt, x_sample, p_prompt, p_sample, emb_ln_g, emb_ln_b, w_in, attn_sink, w_fourier_out, w_attn_out, w_out, ln1_g, ln1_b, peer_w_q, peer_keys, peer_u, peer_v, ple_w_gate, ple_w_proj, ln2_g, ln2_b):
    depth = w_in.shape[0]
    layers = [_prepare_layer(i, w_in, attn_sink, w_fourier_out, w_attn_out, w_out, ln1_g, ln1_b, peer_w_q,
                             peer_keys, peer_u, peer_v, ple_w_gate, ple_w_proj, ln2_g, ln2_b)
              for i in range(depth)]
    emb_ln = (emb_ln_g.reshape(1, D_MODEL).astype(F32), emb_ln_b.reshape(1, D_MODEL).astype(F32))
    mch = jnp.asarray(_channel_dft() / math.sqrt(FOURIER_GROUP_DIM), BF16)
    bias = jnp.asarray(_attn_bias())
    outs = []
    for x, p in ((x_prompt, p_prompt), (x_sample, p_sample)):
        s = x.shape[1]
        cmat, smat = _sequence_dft(s, 1.0 / math.sqrt(s))
        outs.append(_trunk(x, p, (mch, cmat, smat, bias), emb_ln, layers))
    return tuple(outs)
```

```python
import functools
import math

import numpy as np
import jax
import jax.numpy as jnp
from jax import lax
from jax.experimental import pallas as pl
from jax.experimental.pallas import tpu as pltpu

F32 = jnp.float32
BF16 = jnp.bfloat16

D_MODEL = 1024
N_HEADS = 8
N_KV_HEADS = 2
Q_GROUP = N_HEADS // N_KV_HEADS
HEAD_DIM = 64
WINDOW = 128
ATTN_BLOCK = 128
ATTN_WIDTH = N_HEADS * HEAD_DIM
KV_WIDTH = N_KV_HEADS * HEAD_DIM
N_FOURIER_GROUPS = 8
FOURIER_GROUP_DIM = 64
FOURIER_WIDTH = N_FOURIER_GROUPS * FOURIER_GROUP_DIM
GATE_WIDTH = 2 * D_MODEL
IN_WIDTH = FOURIER_WIDTH + ATTN_WIDTH + 2 * KV_WIDTH + GATE_WIDTH
PEER_HEADS = 8
PEER_NKEYS = 128
PEER_N_EXPERTS = PEER_NKEYS * PEER_NKEYS
PEER_TOPK = 16
PEER_KEY_DIM = 128
PLE_DIM = 256
DEPTH = 2
DEEPNORM_ALPHA = (2 * DEPTH) ** 0.25
LN_EPS = 1e-5
GELU_FOLD = 1.0 / math.sqrt(2.0)
NEG_INF = -1e30

LANES = 128
SUBLANES = 8
BF16_ROWS = 16
VMEM_LIMIT = 52 * 1024 * 1024

_O_F = 0
_O_Q = _O_F + FOURIER_WIDTH
_O_K = _O_Q + ATTN_WIDTH
_O_V = _O_K + KV_WIDTH
_O_G = _O_V + KV_WIDTH

_CAND_GROUPS = [(0, 0), (0, 8), (1, 0), (2, 0), (3, 0), (4, 0), (5, 0), (6, 0), (7, 0)]
_CAND_ROWS = 8 * (len(_CAND_GROUPS) + 1)
_BIG_INDEX = 1.0e6


def _params(semantics, flags=None):
    return pltpu.CompilerParams(dimension_semantics=semantics, vmem_limit_bytes=VMEM_LIMIT, flags=flags)


def _layer_norm(x, g, b):
    mu = jnp.mean(x, axis=-1, keepdims=True)
    xc = x - mu
    var = jnp.mean(xc * xc, axis=-1, keepdims=True)
    return xc * lax.rsqrt(var + LN_EPS) * g + b


def _dot(a, b):
    return jnp.dot(a, b, preferred_element_type=F32)


def _dot_nt(a, b):
    return lax.dot_general(a, b, (((1,), (1,)), ((), ())), preferred_element_type=F32)


def _inproj_kernel(*refs, apply_ln):
    if apply_ln:
        x_ref, g_ref, b_ref, w_ref, mch_ref, xn_ref, ab_ref, q_ref, k_ref, v_ref, gt_ref = refs
        x = _layer_norm(x_ref[...], g_ref[...], b_ref[...])
        xn_ref[...] = x
    else:
        x_ref, w_ref, mch_ref, ab_ref, q_ref, k_ref, v_ref, gt_ref = refs
        x = x_ref[...]
    xb = x.astype(BF16)
    f = _dot(xb, w_ref[:, _O_F:_O_Q])
    ab_ref[...] = _dot(f.astype(BF16), mch_ref[...]).astype(BF16)
    q_ref[...] = _dot(xb, w_ref[:, _O_Q:_O_K]).astype(BF16)
    k_ref[...] = _dot(xb, w_ref[:, _O_K:_O_V]).astype(BF16)
    v_ref[...] = _dot(xb, w_ref[:, _O_V:_O_G]).astype(BF16)
    gt_ref[...] = jax.nn.sigmoid(_dot(xb, w_ref[:, _O_G:IN_WIDTH])).astype(BF16)


def _inproj(x, ln, w_in, mch, tile):
    t = x.shape[0]
    apply_ln = ln is not None
    row = lambda width: pl.BlockSpec((tile, width), lambda i: (i, 0))
    full = lambda a: pl.BlockSpec(a.shape, lambda i: (0,) * a.ndim)
    ins = [x] + ([ln[0], ln[1]] if apply_ln else []) + [w_in, mch]
    in_specs = [row(D_MODEL)] + ([full(ln[0]), full(ln[1])] if apply_ln else []) + [full(w_in), full(mch)]
    widths = [2 * FOURIER_WIDTH, ATTN_WIDTH, KV_WIDTH, KV_WIDTH, GATE_WIDTH]
    out_shape = [jax.ShapeDtypeStruct((t, w), BF16) for w in widths]
    out_specs = [row(w) for w in widths]
    if apply_ln:
        out_shape = [jax.ShapeDtypeStruct((t, D_MODEL), F32)] + out_shape
        out_specs = [row(D_MODEL)] + out_specs
    return pl.pallas_call(
        functools.partial(_inproj_kernel, apply_ln=apply_ln),
        grid=(t // tile,),
        in_specs=in_specs,
        out_specs=out_specs,
        out_shape=out_shape,
        compiler_params=_params(("parallel",)),
        name="inproj_ln" if apply_ln else "inproj",
    )(*ins)


def _seqdft_kernel(c_ref, s_ref, a_ref, b_ref, y_ref):
    y_ref[...] = (_dot(c_ref[...], a_ref[...]) + _dot(s_ref[...], b_ref[...])).astype(BF16)


def _seqdft(ab, cmat, smat, tile):
    bsz, s, _ = ab.shape
    return pl.pallas_call(
        _seqdft_kernel,
        grid=(bsz, s // tile),
        in_specs=[
            pl.BlockSpec((tile, s), lambda b, m: (m, 0)),
            pl.BlockSpec((tile, s), lambda b, m: (m, 0)),
            pl.BlockSpec((None, s, FOURIER_WIDTH), lambda b, m: (b, 0, 0)),
            pl.BlockSpec((None, s, FOURIER_WIDTH), lambda b, m: (b, 0, 1)),
        ],
        out_specs=pl.BlockSpec((None, tile, FOURIER_WIDTH), lambda b, m: (b, m, 0)),
        out_shape=jax.ShapeDtypeStruct((bsz, s, FOURIER_WIDTH), BF16),
        compiler_params=_params(("parallel", "parallel")),
        name="seqdft",
    )(cmat, smat, ab, ab)


def _attn_kernel(sink_ref, q_ref, kp_ref, kc_ref, kn_ref, vp_ref, vc_ref, vn_ref, bias_ref, o_ref, *, nsteps, qblocks):
    n = pl.program_id(1)
    col = lax.broadcasted_iota(jnp.int32, (1, 3 * ATTN_BLOCK), 1)
    kall = jnp.concatenate([kp_ref[...], kc_ref[...], kn_ref[...]], axis=0)
    vall = jnp.concatenate([vp_ref[...], vc_ref[...], vn_ref[...]], axis=0)
    vones = jnp.concatenate([vall, jnp.ones_like(vall)], axis=1)
    lane = lax.broadcasted_iota(jnp.int32, (ATTN_BLOCK, KV_WIDTH), 1)
    low = lane < HEAD_DIM
    for j in range(qblocks):
        rows = slice(j * ATTN_BLOCK, (j + 1) * ATTN_BLOCK)
        window = slice(j * ATTN_BLOCK, (j + 3) * ATTN_BLOCK)
        off_edge = None
        if j == 0:
            off_edge = (col < ATTN_BLOCK) & (n == 0)
        if j == qblocks - 1:
            after = (col >= 2 * ATTN_BLOCK) & (n == nsteps - 1)
            off_edge = after if off_edge is None else off_edge | after
        kcat = kall[window]
        vcat = vones[window]
        for g in range(Q_GROUP):
            qg = q_ref[rows, g * KV_WIDTH:(g + 1) * KV_WIDTH]
            halves = []
            for kh in range(N_KV_HEADS):
                h = kh * Q_GROUP + g
                qm = jnp.where(low if kh == 0 else jnp.logical_not(low), qg, jnp.zeros_like(qg))
                s = _dot_nt(qm, kcat) + bias_ref[h]
                if off_edge is not None:
                    s = s + jnp.where(off_edge, NEG_INF, 0.0).astype(F32)
                sink = sink_ref[h]
                m = jnp.maximum(jnp.max(s, axis=-1, keepdims=True), sink)
                p = jnp.exp(s - m)
                pv = _dot(p.astype(BF16), vcat)
                halves.append(pv[:, :KV_WIDTH] / (pv[:, KV_WIDTH:] + jnp.exp(sink - m)))
            o_ref[rows, g * KV_WIDTH:(g + 1) * KV_WIDTH] = jnp.where(low, halves[0], halves[1]).astype(BF16)


def _attention(q, k, v, bias, sink, qblocks):
    bsz, s, _ = q.shape
    nb = s // ATTN_BLOCK
    nsteps = nb // qblocks
    edge_spec = lambda fn: pl.BlockSpec((None, ATTN_BLOCK, KV_WIDTH), fn)
    prev = lambda b, n: (b, jnp.maximum(qblocks * n - 1, 0), 0)
    cur = lambda b, n: (b, n, 0)
    nxt = lambda b, n: (b, jnp.minimum(qblocks * (n + 1), nb - 1), 0)
    mid_spec = pl.BlockSpec((None, qblocks * ATTN_BLOCK, KV_WIDTH), cur)
    return pl.pallas_call(
        functools.partial(_attn_kernel, nsteps=nsteps, qblocks=qblocks),
        grid=(bsz, nsteps),
        in_specs=[
            pl.BlockSpec(memory_space=pltpu.SMEM),
            pl.BlockSpec((None, qblocks * ATTN_BLOCK, ATTN_WIDTH), cur),
            edge_spec(prev), mid_spec, edge_spec(nxt),
            edge_spec(prev), mid_spec, edge_spec(nxt),
            pl.BlockSpec(bias.shape, lambda b, n: (0, 0, 0)),
        ],
        out_specs=pl.BlockSpec((None, qblocks * ATTN_BLOCK, ATTN_WIDTH), cur),
        out_shape=jax.ShapeDtypeStruct((bsz, s, ATTN_WIDTH), BF16),
        compiler_params=_params(("parallel", "parallel")),
        name="window_attn",
    )(sink, q, k, k, k, v, v, v, bias)


def _merge_kernel(x_ref, y_ref, a_ref, gt_ref, wfo_ref, wao_ref, wout_ref, g_ref, b_ref, o_ref):
    f = _dot(y_ref[...], wfo_ref[...])
    a = _dot(a_ref[...], wao_ref[...])
    merged = gt_ref[:, :D_MODEL].astype(F32) * f + gt_ref[:, D_MODEL:].astype(F32) * a
    o = _dot(merged.astype(BF16), wout_ref[...])
    o_ref[...] = _layer_norm(DEEPNORM_ALPHA * x_ref[...] + o, g_ref[...], b_ref[...])


def _merge(x, y, a, gt, wfo, wao, wout, g, b, tile):
    t = x.shape[0]
    row = lambda width: pl.BlockSpec((tile, width), lambda i: (i, 0))
    full = lambda arr: pl.BlockSpec(arr.shape, lambda i: (0,) * arr.ndim)
    return pl.pallas_call(
        _merge_kernel,
        grid=(t // tile,),
        in_specs=[row(D_MODEL), row(FOURIER_WIDTH), row(ATTN_WIDTH), row(GATE_WIDTH),
                  full(wfo), full(wao), full(wout), full(g), full(b)],
        out_specs=row(D_MODEL),
        out_shape=jax.ShapeDtypeStruct((t, D_MODEL), F32),
        compiler_params=_params(("parallel",)),
        name="merge_ln1",
    )(x, y, a, gt, wfo, wao, wout, g, b)


def _extract_sorted(s, ids, count, vals_ref):
    def body(r, carry):
        s, rank = carry
        m = jnp.max(s, axis=0, keepdims=True)
        first = jnp.min(jnp.where(s == m, ids, _BIG_INDEX), axis=0, keepdims=True)
        sel = ids == first
        if vals_ref is not None:
            vals_ref[pl.ds(r, 1), :] = m
        rank = jnp.where(sel, lax.convert_element_type(r, F32), rank)
        s = jnp.where(sel, -jnp.inf, s)
        return s, rank

    rank0 = jnp.full(s.shape, float(count), F32)
    s, rank = lax.fori_loop(0, count, body, (s, rank0))
    return rank, s


def _select_exact(s1, s2, key_ids, cand_ids, cand_mask, v1_scr, v2_scr):
    rank1, _ = _extract_sorted(s1, key_ids, PEER_TOPK, v1_scr)
    rank2, _ = _extract_sorted(s2, key_ids, PEER_TOPK, v2_scr)
    v1 = v1_scr[...]
    v2 = v2_scr[...]
    groups = [v1[r1:r1 + 1] + v2[lo:lo + SUBLANES] for r1, lo in _CAND_GROUPS]
    groups.append(v1[SUBLANES:] + v2[0:1])
    cand = jnp.concatenate(groups, axis=0) + cand_mask
    crank, _ = _extract_sorted(cand, cand_ids, PEER_TOPK, None)
    chosen = crank < float(PEER_TOPK)
    pexp = jnp.where(chosen, jnp.exp(cand - cand[0:1]), 0.0)
    inv_z = GELU_FOLD / jnp.sum(pexp, axis=0, keepdims=True)
    cnt = chosen.astype(F32)
    counts = [jnp.sum(cnt[0:2 * SUBLANES], axis=0, keepdims=True)]
    for gi in range(2, len(_CAND_GROUPS)):
        counts.append(jnp.sum(cnt[gi * SUBLANES:(gi + 1) * SUBLANES], axis=0, keepdims=True))
    tail = cnt[len(_CAND_GROUPS) * SUBLANES:]
    c1 = jnp.zeros_like(s1)
    for r in range(PEER_TOPK):
        cr = counts[r] if r < SUBLANES else tail[r - SUBLANES:r - SUBLANES + 1]
        c1 = jnp.where(rank1 == float(r), cr, c1)
    e1 = jnp.exp(s1 - v1[0:1]) * inv_z
    e2 = jnp.exp(s2 - v2[0:1])
    return e1, c1, e2, rank2


def _merge_exchange_pairs(n):
    pairs = []
    t = max(1, math.ceil(math.log2(n)))
    p = 1 << (t - 1)
    while p > 0:
        q, r, d = 1 << (t - 1), 0, p
        while d > 0:
            pairs.extend((i, i + d) for i in range(n - d) if (i & p) == r)
            d, q, r = q - p, q >> 1, p
        p >>= 1
    return pairs


def _compare_exchange(v, i, j):
    v[i], v[j] = jnp.maximum(v[i], v[j]), jnp.minimum(v[i], v[j])


def _sort_desc(v):
    v = list(v)
    for i, j in _merge_exchange_pairs(len(v)):
        _compare_exchange(v, i, j)
    return v


def _bitonic_merge(v):
    v = list(v)
    d = len(v) // 2
    while d:
        for k in range(len(v)):
            if not k & d:
                _compare_exchange(v, k, k + d)
        d //= 2
    return v


def _across_sublanes(x, op):
    for shift in (4, 2, 1):
        x = op(x, pltpu.roll(x, shift, 0))
    return x


def _top16_of_sublane_lists(lists):
    for shift in (4, 2, 1):
        other = [pltpu.roll(x, shift, 0) for x in lists]
        n = len(lists)
        merged = []
        for k in range(PEER_TOPK):
            mine = lists[k] if k < n else None
            theirs = other[PEER_TOPK - 1 - k] if PEER_TOPK - 1 - k < n else None
            merged.append(mine if theirs is None else theirs if mine is None else jnp.maximum(mine, theirs))
        lists = _bitonic_merge(merged)
    return lists


def _rank_among_sorted(a, v):
    assert len(v) == PEER_TOPK == 16
    m1 = a >= v[7]
    m2 = a >= jnp.where(m1, v[3], v[11])
    m3 = a >= jnp.where(m1, jnp.where(m2, v[1], v[5]), jnp.where(m2, v[9], v[13]))
    t4 = jnp.where(m1,
                   jnp.where(m2, jnp.where(m3, v[0], v[2]), jnp.where(m3, v[4], v[6])),
                   jnp.where(m2, jnp.where(m3, v[8], v[10]), jnp.where(m3, v[12], v[14])))
    m4 = a >= t4
    m5 = a >= v[15]
    rank = (jnp.where(m1, 0.0, 8.0) + jnp.where(m2, 0.0, 4.0)) + (jnp.where(m3, 0.0, 2.0) + jnp.where(m4, 0.0, 1.0))
    return rank + jnp.where(m5, 0.0, 1.0)


def _select_sorted(s1, s2, cand_mask):
    nv = PEER_NKEYS // SUBLANES
    a1 = [s1[j * SUBLANES:(j + 1) * SUBLANES] for j in range(nv)]
    a2 = [s2[j * SUBLANES:(j + 1) * SUBLANES] for j in range(nv)]
    v1 = _top16_of_sublane_lists(_sort_desc(a1))
    v2 = _top16_of_sublane_lists(_sort_desc(a2))
    sub = lax.broadcasted_iota(jnp.int32, (SUBLANES, LANES), 0)

    def by_sublane(vals):
        out = vals[SUBLANES - 1]
        for j in range(SUBLANES - 2, -1, -1):
            out = jnp.where(sub == j, vals[j], out)
        return out

    v2_nat = {0: by_sublane(v2[:SUBLANES]), SUBLANES: by_sublane(v2[SUBLANES:])}
    groups = [v1[r1] + v2_nat[lo] for r1, lo in _CAND_GROUPS]
    groups.append(by_sublane(v1[SUBLANES:]) + v2[0])
    groups = [g + cand_mask[i * SUBLANES:(i + 1) * SUBLANES] for i, g in enumerate(groups)]
    top = _top16_of_sublane_lists(_sort_desc(groups))
    thr = top[PEER_TOPK - 1]
    z = None
    for k in range(PEER_TOPK):
        term = jnp.exp(top[k] - top[0])
        z = term if z is None else z + term
    inv_z = GELU_FOLD / z
    picked = [jnp.where(g >= thr, 1.0, 0.0) for g in groups[:-1]]
    add = lambda a, b: a + b
    counts = [_across_sublanes(picked[0] + picked[1], add)]
    counts += [_across_sublanes(picked[r + 1], add) for r in range(1, SUBLANES)]
    counts += [jnp.where(v1[r] + v2[0] >= thr, 1.0, 0.0) for r in range(SUBLANES, PEER_TOPK)]
    e1, c1, e2, r2 = [], [], [], []
    c_total = None
    r_total = None
    for j in range(nv):
        c = jnp.zeros((SUBLANES, LANES), F32)
        for k in range(PEER_TOPK - 1, -1, -1):
            c = jnp.where(a1[j] >= v1[k], counts[k], c)
        r = _rank_among_sorted(a2[j], v2)
        c1.append(c)
        r2.append(r)
        e1.append(jnp.exp(a1[j] - v1[0]) * inv_z)
        e2.append(jnp.exp(a2[j] - v2[0]))
        c_total = c if c_total is None else c_total + c
        r_total = (float(PEER_TOPK) - r) if r_total is None else r_total + (float(PEER_TOPK) - r)
    c_total = _across_sublanes(c_total, add)
    r_total = _across_sublanes(r_total, add)
    distinct_total = float(PEER_TOPK * (PEER_TOPK + 1) // 2)
    bad = jnp.where((c_total != float(PEER_TOPK)) | (r_total != distinct_total), 1.0, 0.0)
    cat = lambda parts: jnp.concatenate(parts, axis=0)
    return cat(e1), cat(c1), cat(e2), cat(r2), bad


def _select_kernel(x_ref, pq_ref, keys_ref, meta_ref, xt_ref, e1_ref, c1_ref, e2_ref, r2_ref,
                   q_scr, v1_scr, v2_scr, *, tile):
    x = x_ref[...]
    xt_ref[...] = x.T.astype(BF16)
    q = _dot(x.astype(BF16), pq_ref[...])
    for hp in range(2 * PEER_HEADS):
        q_scr[hp] = q[:, hp * PEER_KEY_DIM:(hp + 1) * PEER_KEY_DIM].astype(BF16)

    chunks = [slice(c * LANES, (c + 1) * LANES) for c in range(tile // LANES)]

    def scores(h):
        return tuple(_dot_nt(keys_ref[h, half], q_scr[2 * h + half, tok, :]) for tok in chunks for half in (0, 1))

    def head_body(h, s_all):
        s_next = scores(jnp.minimum(h + 1, PEER_HEADS - 1))

        def emit(tok, e1, c1, e2, r2):
            e1_ref[h, :, tok] = e1
            c1_ref[h, :, tok] = c1
            e2_ref[h, :, tok] = e2.astype(BF16)
            r2_ref[h, :, tok] = r2.astype(BF16)

        flags = []
        for c, tok in enumerate(chunks):
            e1, c1, e2, r2, bad = _select_sorted(s_all[2 * c], s_all[2 * c + 1], meta_ref[1])
            emit(tok, e1, c1, e2, r2)
            flags.append(jnp.max(bad))
        for c, tok in enumerate(chunks):
            @pl.when(flags[c] > 0.0)
            def _():
                key_ids = lax.broadcasted_iota(jnp.int32, (PEER_NKEYS, LANES), 0).astype(F32)
                emit(tok, *_select_exact(s_all[2 * c], s_all[2 * c + 1], key_ids, meta_ref[0], meta_ref[1],
                                         v1_scr, v2_scr))
        return s_next

    lax.fori_loop(0, PEER_HEADS, head_body, scores(0))


def _cand_meta():
    ids = np.full((_CAND_ROWS, LANES), _BIG_INDEX, np.float32)
    mask = np.full((_CAND_ROWS, LANES), -np.inf, np.float32)
    for gi, (r1, lo) in enumerate(_CAND_GROUPS):
        for j in range(SUBLANES):
            r2 = lo + j
            if (r1 + 1) * (r2 + 1) <= PEER_TOPK:
                ids[gi * SUBLANES + j] = r1 * PEER_TOPK + r2
                mask[gi * SUBLANES + j] = 0.0
    base = len(_CAND_GROUPS) * SUBLANES
    for j in range(SUBLANES):
        ids[base + j] = (SUBLANES + j) * PEER_TOPK
        mask[base + j] = 0.0
    return np.stack([ids, mask])


def _peer_select(x, pq, keys, tile):
    t = x.shape[0]
    meta = jnp.asarray(_cand_meta())
    dense = lambda dtype: jax.ShapeDtypeStruct((PEER_HEADS, PEER_NKEYS, t), dtype)
    dense_spec = pl.BlockSpec((PEER_HEADS, PEER_NKEYS, tile), lambda i: (0, 0, i))
    full = lambda arr: pl.BlockSpec(arr.shape, lambda i: (0,) * arr.ndim)
    return pl.pallas_call(
        functools.partial(_select_kernel, tile=tile),
        grid=(t // tile,),
        in_specs=[pl.BlockSpec((tile, D_MODEL), lambda i: (i, 0)), full(pq), full(keys), full(meta)],
        out_specs=[pl.BlockSpec((D_MODEL, tile), lambda i: (0, i))] + [dense_spec] * 4,
        out_shape=[jax.ShapeDtypeStruct((D_MODEL, t), BF16), dense(F32), dense(F32), dense(BF16), dense(BF16)],
        scratch_shapes=[
            pltpu.VMEM((2 * PEER_HEADS, tile, PEER_KEY_DIM), BF16),
            pltpu.VMEM((PEER_TOPK, LANES), F32),
            pltpu.VMEM((PEER_TOPK, LANES), F32),
        ],
        compiler_params=_params(("parallel",)),
        name="peer_select",
    )(x, pq, keys, meta)


def _activations(xt_ref, u_ref, act_ref):
    z = _dot(u_ref[...], xt_ref[...])
    act_ref[...] = (z + z * lax.erf(z)).astype(BF16)


def _gate_activations(block, e1_ref, c1_ref, e2_ref, r2_ref, act_ref, w_ref, rows_per_block):
    tile = act_ref.shape[1]
    for jj in range(rows_per_block):
        j = block * rows_per_block + jj
        c1 = [jnp.broadcast_to(c1_ref[h, pl.ds(j, 1), :], (BF16_ROWS, tile)).astype(BF16)
              for h in range(PEER_HEADS)]
        e1 = [jnp.broadcast_to(e1_ref[h, pl.ds(j, 1), :], (BF16_ROWS, tile)).astype(BF16)
              for h in range(PEER_HEADS)]
        for g in range(PEER_NKEYS // BF16_ROWS):
            keys = slice(g * BF16_ROWS, (g + 1) * BF16_ROWS)
            gate = None
            for h in range(PEER_HEADS):
                picked = jnp.where(r2_ref[h, keys, :] < c1[h], e2_ref[h, keys, :], jnp.zeros((), BF16))
                term = picked * e1[h]
                gate = term if gate is None else gate + term
            rows = slice(jj * PEER_NKEYS + g * BF16_ROWS, jj * PEER_NKEYS + (g + 1) * BF16_ROWS)
            w_ref[rows, :] = gate * act_ref[rows, :]


def _dense_kernel(xt_ref, e1_ref, c1_ref, e2_ref, r2_ref, u_ref, vtp_ref, *refs, rows_per_block, nb):
    vt_refs, o_ref = refs[:nb], refs[nb]
    act_scr, w_scr = refs[nb + 1:2 * nb + 1], refs[2 * nb + 1:]
    e = pl.program_id(1)
    nrows = rows_per_block * PEER_NKEYS

    @pl.when(e == 0)
    def _():
        o_ref[...] = jnp.zeros_like(o_ref)
        w_scr[nb - 1][...] = jnp.zeros_like(w_scr[nb - 1])

    def activations(i):
        _activations(xt_ref, u_ref.at[i * nrows:(i + 1) * nrows, :], act_scr[i])

    activations(0)
    o_ref[...] += _dot(vtp_ref[...], w_scr[nb - 1][...])
    for i in range(nb):
        _gate_activations(nb * e + i, e1_ref, c1_ref, e2_ref, r2_ref, act_scr[i], w_scr[i], rows_per_block)
        if i + 1 < nb:
            activations(i + 1)
            o_ref[...] += _dot(vt_refs[i][...], w_scr[i][...])

    @pl.when(e == pl.num_programs(1) - 1)
    def _():
        o_ref[...] += _dot(vt_refs[nb - 1][...], w_scr[nb - 1][...])


def _peer_dense(xt, e1, c1, e2, r2, u, vt, tile, eblock, nb):
    t = xt.shape[1]
    nsteps = PEER_N_EXPERTS // (nb * eblock)
    row_spec = pl.BlockSpec((PEER_HEADS, PEER_NKEYS, tile), lambda i, e: (0, 0, i))
    vt_spec = lambda fn: pl.BlockSpec((D_MODEL, eblock), fn)

    def vt_block(i, e, k):
        block = nb * e + k
        return (0, jnp.where(e == nsteps - 1, block, 0) if k == nb - 1 else block)

    return pl.pallas_call(
        functools.partial(_dense_kernel, rows_per_block=eblock // PEER_NKEYS, nb=nb),
        grid=(t // tile, nsteps),
        in_specs=[pl.BlockSpec((D_MODEL, tile), lambda i, e: (0, i))] + [row_spec] * 4 + [
            pl.BlockSpec((nb * eblock, D_MODEL), lambda i, e: (e, 0)),
            vt_spec(lambda i, e: (0, jnp.maximum(nb * e - 1, 0))),
        ] + [vt_spec(functools.partial(vt_block, k=k)) for k in range(nb)],
        out_specs=pl.BlockSpec((D_MODEL, tile), lambda i, e: (0, i)),
        out_shape=jax.ShapeDtypeStruct((D_MODEL, t), F32),
        scratch_shapes=[pltpu.VMEM((eblock, tile), BF16)] * (2 * nb),
        compiler_params=_params(("parallel", "arbitrary")),
        name="peer_dense",
    )(xt, e1, c1, e2, r2, u, vt, *([vt] * nb))


def _final_kernel(x_ref, cmt_ref, p_ref, wg_ref, wp_ref, g_ref, b_ref, o_ref):
    x = x_ref[...]
    gate = jax.nn.sigmoid(_dot(x.astype(BF16), wg_ref[...]))
    ple = gate * _dot(p_ref[...].astype(BF16), wp_ref[...])
    o_ref[...] = _layer_norm(DEEPNORM_ALPHA * x + cmt_ref[...].T + ple, g_ref[...], b_ref[...])


def _final(x, cmt, p_all, layer, wg, wp, g, b, tile):
    t = x.shape[0]
    full = lambda arr: pl.BlockSpec(arr.shape, lambda i: (0,) * arr.ndim)
    return pl.pallas_call(
        _final_kernel,
        grid=(t // tile,),
        in_specs=[pl.BlockSpec((tile, D_MODEL), lambda i: (i, 0)),
                  pl.BlockSpec((D_MODEL, tile), lambda i: (0, i)),
                  pl.BlockSpec((None, tile, PLE_DIM), lambda i: (layer, i, 0)),
                  full(wg), full(wp), full(g), full(b)],
        out_specs=pl.BlockSpec((tile, D_MODEL), lambda i: (i, 0)),
        out_shape=jax.ShapeDtypeStruct((t, D_MODEL), F32),
        compiler_params=_params(("parallel",)),
        name="ple_ln2",
    )(x, cmt, p_all, wg, wp, g, b)


def _channel_dft():
    c = np.arange(FOURIER_GROUP_DIM)
    ang = 2.0 * np.pi * ((c[:, None] * c[None, :]) % FOURIER_GROUP_DIM) / FOURIER_GROUP_DIM
    eye = np.eye(N_FOURIER_GROUPS)
    return np.concatenate([np.kron(eye, np.cos(ang)), np.kron(eye, -np.sin(ang))], axis=1)


def _sequence_dft(s, scale):
    s_lo = 64
    s_hi = s // s_lo
    k = np.arange(s)
    ang_hi = 2.0 * np.pi * ((k[:, None] * np.arange(s_hi)[None, :]) % s_hi) / s_hi
    ang_lo = 2.0 * np.pi * ((k[:, None] * np.arange(s_lo)[None, :]) % s) / s
    ch = jnp.asarray(np.cos(ang_hi) * scale, F32)[:, :, None]
    sh = jnp.asarray(np.sin(ang_hi) * scale, F32)[:, :, None]
    cl = jnp.asarray(np.cos(ang_lo), F32)[:, None, :]
    sl = jnp.asarray(np.sin(ang_lo), F32)[:, None, :]
    cmat = (ch * cl - sh * sl).reshape(s, s).astype(BF16)
    smat = (sh * cl + ch * sl).reshape(s, s).astype(BF16)
    return cmat, smat


def _attn_bias():
    qi = np.arange(ATTN_BLOCK)
    kj = np.arange(3 * ATTN_BLOCK)
    rel = np.abs(qi[:, None] + ATTN_BLOCK - kj[None, :]).astype(np.float64)
    slopes = np.exp2(-8.0 * np.arange(1, N_HEADS + 1) / N_HEADS)
    bias = np.where(rel[None] <= WINDOW, -slopes[:, None, None] * rel[None], NEG_INF)
    return bias.astype(np.float32)


def _pick_tile(n, pref):
    tile = min(n, pref)
    assert n % tile == 0, (n, tile)
    return tile


def _tiles(t, s):
    return dict(
        rows=_pick_tile(t, 512),
        dft_rows=_pick_tile(s, 512),
        select_tokens=_pick_tile(t, 4 * LANES),
        dense_tokens=_pick_tile(t, 512),
        dense_experts=512,
        dense_blocks=2,
        attn_qblocks=_pick_tile(s // ATTN_BLOCK, 8),
    )


def _trunk(x, p_all, consts, emb_ln, layers):
    bsz, s, _ = x.shape
    t = bsz * s
    mch, cmat, smat, bias = consts
    tiles = _tiles(t, s)
    x = x.reshape(t, D_MODEL)
    p_all = p_all.reshape(p_all.shape[0], t, PLE_DIM)
    for i, lw in enumerate(layers):
        if i == 0:
            x, ab, q, k, v, gt = _inproj(x, emb_ln, lw["w_in"], mch, tiles["rows"])
        else:
            ab, q, k, v, gt = _inproj(x, None, lw["w_in"], mch, tiles["rows"])
        y = _seqdft(ab.reshape(bsz, s, 2 * FOURIER_WIDTH), cmat, smat, tiles["dft_rows"])
        a = _attention(q.reshape(bsz, s, ATTN_WIDTH), k.reshape(bsz, s, KV_WIDTH),
                       v.reshape(bsz, s, KV_WIDTH), bias, lw["sink"], tiles["attn_qblocks"])
        x = _merge(x, y.reshape(t, FOURIER_WIDTH), a.reshape(t, ATTN_WIDTH), gt,
                   lw["w_fo"], lw["w_ao"], lw["w_out"], lw["ln1_g"], lw["ln1_b"], tiles["rows"])
        xt, e1, c1, e2, r2 = _peer_select(x, lw["pq"], lw["keys"], tiles["select_tokens"])
        cmt = _peer_dense(xt, e1, c1, e2, r2, lw["u"], lw["vt"], tiles["dense_tokens"],
                          tiles["dense_experts"], tiles["dense_blocks"])
        x = _final(x, cmt, p_all, i, lw["wg"], lw["wp"], lw["ln2_g"], lw["ln2_b"], tiles["rows"])
    return x.reshape(bsz, s, D_MODEL)


def _prepare_layer(i, w_in, attn_sink, w_fourier_out, w_attn_out, w_out, ln1_g, ln1_b, peer_w_q,
                   peer_keys, peer_u, peer_v, ple_w_gate, ple_w_proj, ln2_g, ln2_b):
    w = w_in[i]
    wq = w[:, _O_Q:_O_K].reshape(D_MODEL, N_KV_HEADS, Q_GROUP, HEAD_DIM).transpose(0, 2, 1, 3)
    wq = wq.reshape(D_MODEL, ATTN_WIDTH) * (HEAD_DIM ** -0.5)
    w_perm = jnp.concatenate([w[:, :_O_Q], wq, w[:, _O_K:]], axis=1).astype(BF16)
    wao = w_attn_out[i].reshape(N_KV_HEADS, Q_GROUP, HEAD_DIM, D_MODEL).transpose(1, 0, 2, 3)
    row = lambda a: a[i].reshape(1, D_MODEL).astype(F32)
    return dict(
        w_in=w_perm, sink=attn_sink[i].astype(F32),
        w_fo=w_fourier_out[i].astype(BF16), w_ao=wao.reshape(ATTN_WIDTH, D_MODEL).astype(BF16),
        w_out=w_out[i].astype(BF16), ln1_g=row(ln1_g), ln1_b=row(ln1_b),
        pq=peer_w_q[i].astype(BF16), keys=peer_keys[i].astype(BF16),
        u=(peer_u[i] * (1.0 / math.sqrt(2.0))).astype(BF16), vt=peer_v[i].astype(BF16).T,
        wg=ple_w_gate[i].astype(BF16), wp=ple_w_proj[i].astype(BF16), ln2_g=row(ln2_g), ln2_b=row(ln2_b),
    )


def kernel(x_prompt, x_sample, p_prompt, p_sample, emb_ln_g, emb_ln_b, w_in, attn_sink, w_fourier_out, w_attn_out, w_out, ln1_g, ln1_b, peer_w_q, peer_keys, peer_u, peer_v, ple_w_gate, ple_w_proj, ln2_g, ln2_b):
    depth = w_in.shape[0]
    layers = [_prepare_layer(i, w_in, attn_sink, w_fourier_out, w_attn_out, w_out, ln1_g, ln1_b, peer_w_q,
                             peer_keys, peer_u, peer_v, ple_w_gate, ple_w_proj, ln2_g, ln2_b)
              for i in range(depth)]
    emb_ln = (emb_ln_g.reshape(1, D_MODEL).astype(F32), emb_ln_b.reshape(1, D_MODEL).astype(F32))
    mch = jnp.asarray(_channel_dft() / math.sqrt(FOURIER_GROUP_DIM), BF16)
    bias = jnp.asarray(_attn_bias())
    outs = []
    for x, p in ((x_prompt, p_prompt), (x_sample, p_sample)):
        s = x.shape[1]
        cmat, smat = _sequence_dft(s, 1.0 / math.sqrt(s))
        outs.append(_trunk(x, p, (mch, cmat, smat, bias), emb_ln, layers))
    return tuple(outs)
```

```python
import functools
import math

import numpy as np
import jax
import jax.numpy as jnp
from jax import lax
from jax.experimental import pallas as pl
from jax.experimental.pallas import tpu as pltpu

F32 = jnp.float32
BF16 = jnp.bfloat16

D_MODEL = 1024
N_HEADS = 8
N_KV_HEADS = 2
Q_GROUP = N_HEADS // N_KV_HEADS
HEAD_DIM = 64
WINDOW = 128
ATTN_BLOCK = 128
ATTN_WIDTH = N_HEADS * HEAD_DIM
KV_WIDTH = N_KV_HEADS * HEAD_DIM
N_FOURIER_GROUPS = 8
FOURIER_GROUP_DIM = 64
FOURIER_WIDTH = N_FOURIER_GROUPS * FOURIER_GROUP_DIM
GATE_WIDTH = 2 * D_MODEL
IN_WIDTH = FOURIER_WIDTH + ATTN_WIDTH + 2 * KV_WIDTH + GATE_WIDTH
PEER_HEADS = 8
PEER_NKEYS = 128
PEER_N_EXPERTS = PEER_NKEYS * PEER_NKEYS
PEER_TOPK = 16
PEER_KEY_DIM = 128
PLE_DIM = 256
DEPTH = 2
DEEPNORM_ALPHA = (2 * DEPTH) ** 0.25
LN_EPS = 1e-5
GELU_FOLD = 1.0 / math.sqrt(2.0)
NEG_INF = -1e30

LANES = 128
SUBLANES = 8
BF16_ROWS = 16
VMEM_LIMIT = 52 * 1024 * 1024

_O_F = 0
_O_Q = _O_F + FOURIER_WIDTH
_O_K = _O_Q + ATTN_WIDTH
_O_V = _O_K + KV_WIDTH
_O_G = _O_V + KV_WIDTH

_CAND_GROUPS = [(0, 0), (0, 8), (1, 0), (2, 0), (3, 0), (4, 0), (5, 0), (6, 0), (7, 0)]
_CAND_ROWS = 8 * (len(_CAND_GROUPS) + 1)
_BIG_INDEX = 1.0e6


def _params(semantics, flags=None):
    return pltpu.CompilerParams(dimension_semantics=semantics, vmem_limit_bytes=VMEM_LIMIT, flags=flags)


def _layer_norm(x, g, b):
    mu = jnp.mean(x, axis=-1, keepdims=True)
    xc = x - mu
    var = jnp.mean(xc * xc, axis=-1, keepdims=True)
    return xc * lax.rsqrt(var + LN_EPS) * g + b


def _dot(a, b):
    return jnp.dot(a, b, preferred_element_type=F32)


def _dot_nt(a, b):
    return lax.dot_general(a, b, (((1,), (1,)), ((), ())), preferred_element_type=F32)


def _inproj_kernel(*refs, apply_ln):
    if apply_ln:
        x_ref, g_ref, b_ref, w_ref, mch_ref, xn_ref, ab_ref, q_ref, k_ref, v_ref, gt_ref = refs
        x = _layer_norm(x_ref[...], g_ref[...], b_ref[...])
        xn_ref[...] = x
    else:
        x_ref, w_ref, mch_ref, ab_ref, q_ref, k_ref, v_ref, gt_ref = refs
        x = x_ref[...]
    xb = x.astype(BF16)
    f = _dot(xb, w_ref[:, _O_F:_O_Q])
    ab_ref[...] = _dot(f.astype(BF16), mch_ref[...]).astype(BF16)
    q_ref[...] = _dot(xb, w_ref[:, _O_Q:_O_K]).astype(BF16)
    k_ref[...] = _dot(xb, w_ref[:, _O_K:_O_V]).astype(BF16)
    v_ref[...] = _dot(xb, w_ref[:, _O_V:_O_G]).astype(BF16)
    gt_ref[...] = jax.nn.sigmoid(_dot(xb, w_ref[:, _O_G:IN_WIDTH])).astype(BF16)


def _inproj(x, ln, w_in, mch, tile):
    t = x.shape[0]
    apply_ln = ln is not None
    row = lambda width: pl.BlockSpec((tile, width), lambda i: (i, 0))
    full = lambda a: pl.BlockSpec(a.shape, lambda i: (0,) * a.ndim)
    ins = [x] + ([ln[0], ln[1]] if apply_ln else []) + [w_in, mch]
    in_specs = [row(D_MODEL)] + ([full(ln[0]), full(ln[1])] if apply_ln else []) + [full(w_in), full(mch)]
    widths = [2 * FOURIER_WIDTH, ATTN_WIDTH, KV_WIDTH, KV_WIDTH, GATE_WIDTH]
    out_shape = [jax.ShapeDtypeStruct((t, w), BF16) for w in widths]
    out_specs = [row(w) for w in widths]
    if apply_ln:
        out_shape = [jax.ShapeDtypeStruct((t, D_MODEL), F32)] + out_shape
        out_specs = [row(D_MODEL)] + out_specs
    return pl.pallas_call(
        functools.partial(_inproj_kernel, apply_ln=apply_ln),
        grid=(t // tile,),
        in_specs=in_specs,
        out_specs=out_specs,
        out_shape=out_shape,
        compiler_params=_params(("parallel",)),
        name="inproj_ln" if apply_ln else "inproj",
    )(*ins)


def _seqdft_kernel(c_ref, s_ref, a_ref, b_ref, y_ref):
    y_ref[...] = (_dot(c_ref[...], a_ref[...]) + _dot(s_ref[...], b_ref[...])).astype(BF16)


def _seqdft(ab, cmat, smat, tile):
    bsz, s, _ = ab.shape
    return pl.pallas_call(
        _seqdft_kernel,
        grid=(bsz, s // tile),
        in_specs=[
            pl.BlockSpec((tile, s), lambda b, m: (m, 0)),
            pl.BlockSpec((tile, s), lambda b, m: (m, 0)),
            pl.BlockSpec((None, s, FOURIER_WIDTH), lambda b, m: (b, 0, 0)),
            pl.BlockSpec((None, s, FOURIER_WIDTH), lambda b, m: (b, 0, 1)),
        ],
        out_specs=pl.BlockSpec((None, tile, FOURIER_WIDTH), lambda b, m: (b, m, 0)),
        out_shape=jax.ShapeDtypeStruct((bsz, s, FOURIER_WIDTH), BF16),
        compiler_params=_params(("parallel", "parallel")),
        name="seqdft",
    )(cmat, smat, ab, ab)


def _attn_kernel(sink_ref, q_ref, kp_ref, kc_ref, kn_ref, vp_ref, vc_ref, vn_ref, bias_ref, o_ref, *, nsteps, qblocks):
    n = pl.program_id(1)
    col = lax.broadcasted_iota(jnp.int32, (1, 3 * ATTN_BLOCK), 1)
    kall = jnp.concatenate([kp_ref[...], kc_ref[...], kn_ref[...]], axis=0)
    vall = jnp.concatenate([vp_ref[...], vc_ref[...], vn_ref[...]], axis=0)
    vones = jnp.concatenate([vall, jnp.ones_like(vall)], axis=1)
    lane = lax.broadcasted_iota(jnp.int32, (ATTN_BLOCK, KV_WIDTH), 1)
    low = lane < HEAD_DIM
    for j in range(qblocks):
        rows = slice(j * ATTN_BLOCK, (j + 1) * ATTN_BLOCK)
        window = slice(j * ATTN_BLOCK, (j + 3) * ATTN_BLOCK)
        off_edge = None
        if j == 0:
            off_edge = (col < ATTN_BLOCK) & (n == 0)
        if j == qblocks - 1:
            after = (col >= 2 * ATTN_BLOCK) & (n == nsteps - 1)
            off_edge = after if off_edge is None else off_edge | after
        kcat = kall[window]
        vcat = vones[window]
        for g in range(Q_GROUP):
            qg = q_ref[rows, g * KV_WIDTH:(g + 1) * KV_WIDTH]
            halves = []
            for kh in range(N_KV_HEADS):
                h = kh * Q_GROUP + g
                qm = jnp.where(low if kh == 0 else jnp.logical_not(low), qg, jnp.zeros_like(qg))
                s = _dot_nt(qm, kcat) + bias_ref[h]
                if off_edge is not None:
                    s = s + jnp.where(off_edge, NEG_INF, 0.0).astype(F32)
                sink = sink_ref[h]
                m = jnp.maximum(jnp.max(s, axis=-1, keepdims=True), sink)
                p = jnp.exp(s - m)
                pv = _dot(p.astype(BF16), vcat)
                halves.append(pv[:, :KV_WIDTH] / (pv[:, KV_WIDTH:] + jnp.exp(sink - m)))
            o_ref[rows, g * KV_WIDTH:(g + 1) * KV_WIDTH] = jnp.where(low, halves[0], halves[1]).astype(BF16)


def _attention(q, k, v, bias, sink, qblocks):
    bsz, s, _ = q.shape
    nb = s // ATTN_BLOCK
    nsteps = nb // qblocks
    edge_spec = lambda fn: pl.BlockSpec((None, ATTN_BLOCK, KV_WIDTH), fn)
    prev = lambda b, n: (b, jnp.maximum(qblocks * n - 1, 0), 0)
    cur = lambda b, n: (b, n, 0)
    nxt = lambda b, n: (b, jnp.minimum(qblocks * (n + 1), nb - 1), 0)
    mid_spec = pl.BlockSpec((None, qblocks * ATTN_BLOCK, KV_WIDTH), cur)
    return pl.pallas_call(
        functools.partial(_attn_kernel, nsteps=nsteps, qblocks=qblocks),
        grid=(bsz, nsteps),
        in_specs=[
            pl.BlockSpec(memory_space=pltpu.SMEM),
            pl.BlockSpec((None, qblocks * ATTN_BLOCK, ATTN_WIDTH), cur),
            edge_spec(prev), mid_spec, edge_spec(nxt),
            edge_spec(prev), mid_spec, edge_spec(nxt),
            pl.BlockSpec(bias.shape, lambda b, n: (0, 0, 0)),
        ],
        out_specs=pl.BlockSpec((None, qblocks * ATTN_BLOCK, ATTN_WIDTH), cur),
        out_shape=jax.ShapeDtypeStruct((bsz, s, ATTN_WIDTH), BF16),
        compiler_params=_params(("parallel", "parallel")),
        name="window_attn",
    )(sink, q, k, k, k, v, v, v, bias)


def _merge_kernel(x_ref, y_ref, a_ref, gt_ref, wfo_ref, wao_ref, wout_ref, g_ref, b_ref, o_ref):
    half = x_ref.shape[0] // 2
    for rows in (slice(0, half), slice(half, 2 * half)):
        f = _dot(y_ref[rows, :], wfo_ref[...])
        a = _dot(a_ref[rows, :], wao_ref[...])
        merged = gt_ref[rows, :D_MODEL].astype(F32) * f + gt_ref[rows, D_MODEL:].astype(F32) * a
        o = _dot(merged.astype(BF16), wout_ref[...])
        o_ref[rows, :] = _layer_norm(DEEPNORM_ALPHA * x_ref[rows, :] + o, g_ref[...], b_ref[...])


def _merge(x, y, a, gt, wfo, wao, wout, g, b, tile):
    t = x.shape[0]
    row = lambda width: pl.BlockSpec((tile, width), lambda i: (i, 0))
    full = lambda arr: pl.BlockSpec(arr.shape, lambda i: (0,) * arr.ndim)
    return pl.pallas_call(
        _merge_kernel,
        grid=(t // tile,),
        in_specs=[row(D_MODEL), row(FOURIER_WIDTH), row(ATTN_WIDTH), row(GATE_WIDTH),
                  full(wfo), full(wao), full(wout), full(g), full(b)],
        out_specs=row(D_MODEL),
        out_shape=jax.ShapeDtypeStruct((t, D_MODEL), F32),
        compiler_params=_params(("parallel",)),
        name="merge_ln1",
    )(x, y, a, gt, wfo, wao, wout, g, b)


def _extract_sorted(s, ids, count, vals_ref):
    def body(r, carry):
        s, rank = carry
        m = jnp.max(s, axis=0, keepdims=True)
        first = jnp.min(jnp.where(s == m, ids, _BIG_INDEX), axis=0, keepdims=True)
        sel = ids == first
        if vals_ref is not None:
            vals_ref[pl.ds(r, 1), :] = m
        rank = jnp.where(sel, lax.convert_element_type(r, F32), rank)
        s = jnp.where(sel, -jnp.inf, s)
        return s, rank

    rank0 = jnp.full(s.shape, float(count), F32)
    s, rank = lax.fori_loop(0, count, body, (s, rank0))
    return rank, s


def _select_exact(s1, s2, key_ids, cand_ids, cand_mask, v1_scr, v2_scr):
    rank1, _ = _extract_sorted(s1, key_ids, PEER_TOPK, v1_scr)
    rank2, _ = _extract_sorted(s2, key_ids, PEER_TOPK, v2_scr)
    v1 = v1_scr[...]
    v2 = v2_scr[...]
    groups = [v1[r1:r1 + 1] + v2[lo:lo + SUBLANES] for r1, lo in _CAND_GROUPS]
    groups.append(v1[SUBLANES:] + v2[0:1])
    cand = jnp.concatenate(groups, axis=0) + cand_mask
    crank, _ = _extract_sorted(cand, cand_ids, PEER_TOPK, None)
    chosen = crank < float(PEER_TOPK)
    pexp = jnp.where(chosen, jnp.exp(cand - cand[0:1]), 0.0)
    inv_z = GELU_FOLD / jnp.sum(pexp, axis=0, keepdims=True)
    cnt = chosen.astype(F32)
    counts = [jnp.sum(cnt[0:2 * SUBLANES], axis=0, keepdims=True)]
    for gi in range(2, len(_CAND_GROUPS)):
        counts.append(jnp.sum(cnt[gi * SUBLANES:(gi + 1) * SUBLANES], axis=0, keepdims=True))
    tail = cnt[len(_CAND_GROUPS) * SUBLANES:]
    c1 = jnp.zeros_like(s1)
    for r in range(PEER_TOPK):
        cr = counts[r] if r < SUBLANES else tail[r - SUBLANES:r - SUBLANES + 1]
        c1 = jnp.where(rank1 == float(r), cr, c1)
    e1 = jnp.exp(s1 - v1[0:1]) * inv_z
    e2 = jnp.exp(s2 - v2[0:1])
    return e1, c1, e2, rank2


def _merge_exchange_pairs(n):
    pairs = []
    t = max(1, math.ceil(math.log2(n)))
    p = 1 << (t - 1)
    while p > 0:
        q, r, d = 1 << (t - 1), 0, p
        while d > 0:
            pairs.extend((i, i + d) for i in range(n - d) if (i & p) == r)
            d, q, r = q - p, q >> 1, p
        p >>= 1
    return pairs


def _compare_exchange(v, i, j):
    v[i], v[j] = jnp.maximum(v[i], v[j]), jnp.minimum(v[i], v[j])


def _sort_desc(v):
    v = list(v)
    for i, j in _merge_exchange_pairs(len(v)):
        _compare_exchange(v, i, j)
    return v


def _bitonic_merge(v):
    v = list(v)
    d = len(v) // 2
    while d:
        for k in range(len(v)):
            if not k & d:
                _compare_exchange(v, k, k + d)
        d //= 2
    return v


def _across_sublanes(x, op):
    for shift in (4, 2, 1):
        x = op(x, pltpu.roll(x, shift, 0))
    return x


def _top16_of_sublane_lists(lists):
    for shift in (4, 2, 1):
        other = [pltpu.roll(x, shift, 0) for x in lists]
        n = len(lists)
        merged = []
        for k in range(PEER_TOPK):
            mine = lists[k] if k < n else None
            theirs = other[PEER_TOPK - 1 - k] if PEER_TOPK - 1 - k < n else None
            merged.append(mine if theirs is None else theirs if mine is None else jnp.maximum(mine, theirs))
        lists = _bitonic_merge(merged)
    return lists


def _rank_among_sorted(a, v):
    assert len(v) == PEER_TOPK == 16
    m1 = a >= v[7]
    m2 = a >= jnp.where(m1, v[3], v[11])
    m3 = a >= jnp.where(m1, jnp.where(m2, v[1], v[5]), jnp.where(m2, v[9], v[13]))
    t4 = jnp.where(m1,
                   jnp.where(m2, jnp.where(m3, v[0], v[2]), jnp.where(m3, v[4], v[6])),
                   jnp.where(m2, jnp.where(m3, v[8], v[10]), jnp.where(m3, v[12], v[14])))
    m4 = a >= t4
    m5 = a >= v[15]
    rank = (jnp.where(m1, 0.0, 8.0) + jnp.where(m2, 0.0, 4.0)) + (jnp.where(m3, 0.0, 2.0) + jnp.where(m4, 0.0, 1.0))
    return rank + jnp.where(m5, 0.0, 1.0)


def _select_sorted(s1, s2, cand_mask):
    nv = PEER_NKEYS // SUBLANES
    a1 = [s1[j * SUBLANES:(j + 1) * SUBLANES] for j in range(nv)]
    a2 = [s2[j * SUBLANES:(j + 1) * SUBLANES] for j in range(nv)]
    v1 = _top16_of_sublane_lists(_sort_desc(a1))
    v2 = _top16_of_sublane_lists(_sort_desc(a2))
    sub = lax.broadcasted_iota(jnp.int32, (SUBLANES, LANES), 0)

    def by_sublane(vals):
        out = vals[SUBLANES - 1]
        for j in range(SUBLANES - 2, -1, -1):
            out = jnp.where(sub == j, vals[j], out)
        return out

    v2_nat = {0: by_sublane(v2[:SUBLANES]), SUBLANES: by_sublane(v2[SUBLANES:])}
    groups = [v1[r1] + v2_nat[lo] for r1, lo in _CAND_GROUPS]
    groups.append(by_sublane(v1[SUBLANES:]) + v2[0])
    groups = [g + cand_mask[i * SUBLANES:(i + 1) * SUBLANES] for i, g in enumerate(groups)]
    top = _top16_of_sublane_lists(_sort_desc(groups))
    thr = top[PEER_TOPK - 1]
    z = None
    for k in range(PEER_TOPK):
        term = jnp.exp(top[k] - top[0])
        z = term if z is None else z + term
    inv_z = GELU_FOLD / z
    picked = [jnp.where(g >= thr, 1.0, 0.0) for g in groups[:-1]]
    add = lambda a, b: a + b
    counts = [_across_sublanes(picked[0] + picked[1], add)]
    counts += [_across_sublanes(picked[r + 1], add) for r in range(1, SUBLANES)]
    counts += [jnp.where(v1[r] + v2[0] >= thr, 1.0, 0.0) for r in range(SUBLANES, PEER_TOPK)]
    e1, c1, e2, r2 = [], [], [], []
    c_total = None
    r_total = None
    for j in range(nv):
        c = jnp.zeros((SUBLANES, LANES), F32)
        for k in range(PEER_TOPK - 1, -1, -1):
            c = jnp.where(a1[j] >= v1[k], counts[k], c)
        r = _rank_among_sorted(a2[j], v2)
        c1.append(c)
        r2.append(r)
        e1.append(jnp.exp(a1[j] - v1[0]) * inv_z)
        e2.append(jnp.exp(a2[j] - v2[0]))
        c_total = c if c_total is None else c_total + c
        r_total = (float(PEER_TOPK) - r) if r_total is None else r_total + (float(PEER_TOPK) - r)
    c_total = _across_sublanes(c_total, add)
    r_total = _across_sublanes(r_total, add)
    distinct_total = float(PEER_TOPK * (PEER_TOPK + 1) // 2)
    bad = jnp.where((c_total != float(PEER_TOPK)) | (r_total != distinct_total), 1.0, 0.0)
    cat = lambda parts: jnp.concatenate(parts, axis=0)
    return cat(e1), cat(c1), cat(e2), cat(r2), bad


def _select_kernel(x_ref, pq_ref, keys_ref, meta_ref, xt_ref, e1_ref, c1_ref, e2_ref, r2_ref,
                   q_scr, v1_scr, v2_scr, *, tile):
    x = x_ref[...]
    xt_ref[...] = x.T.astype(BF16)
    q = _dot(x.astype(BF16), pq_ref[...])
    for hp in range(2 * PEER_HEADS):
        q_scr[hp] = q[:, hp * PEER_KEY_DIM:(hp + 1) * PEER_KEY_DIM].astype(BF16)

    chunks = [slice(c * LANES, (c + 1) * LANES) for c in range(tile // LANES)]

    def scores(h):
        return tuple(_dot_nt(keys_ref[h, half], q_scr[2 * h + half, tok, :]) for tok in chunks for half in (0, 1))

    def head_body(h, s_all):
        s_next = scores(jnp.minimum(h + 1, PEER_HEADS - 1))

        def emit(tok, e1, c1, e2, r2):
            e1_ref[h, :, tok] = e1
            c1_ref[h, :, tok] = c1
            e2_ref[h, :, tok] = e2.astype(BF16)
            r2_ref[h, :, tok] = r2.astype(BF16)

        flags = []
        for c, tok in enumerate(chunks):
            e1, c1, e2, r2, bad = _select_sorted(s_all[2 * c], s_all[2 * c + 1], meta_ref[1])
            emit(tok, e1, c1, e2, r2)
            flags.append(jnp.max(bad))
        for c, tok in enumerate(chunks):
            @pl.when(flags[c] > 0.0)
            def _():
                key_ids = lax.broadcasted_iota(jnp.int32, (PEER_NKEYS, LANES), 0).astype(F32)
                emit(tok, *_select_exact(s_all[2 * c], s_all[2 * c + 1], key_ids, meta_ref[0], meta_ref[1],
                                         v1_scr, v2_scr))
        return s_next

    lax.fori_loop(0, PEER_HEADS, head_body, scores(0))


def _cand_meta():
    ids = np.full((_CAND_ROWS, LANES), _BIG_INDEX, np.float32)
    mask = np.full((_CAND_ROWS, LANES), -np.inf, np.float32)
    for gi, (r1, lo) in enumerate(_CAND_GROUPS):
        for j in range(SUBLANES):
            r2 = lo + j
            if (r1 + 1) * (r2 + 1) <= PEER_TOPK:
                ids[gi * SUBLANES + j] = r1 * PEER_TOPK + r2
                mask[gi * SUBLANES + j] = 0.0
    base = len(_CAND_GROUPS) * SUBLANES
    for j in range(SUBLANES):
        ids[base + j] = (SUBLANES + j) * PEER_TOPK
        mask[base + j] = 0.0
    return np.stack([ids, mask])


def _peer_select(x, pq, keys, tile):
    t = x.shape[0]
    meta = jnp.asarray(_cand_meta())
    dense = lambda dtype: jax.ShapeDtypeStruct((PEER_HEADS, PEER_NKEYS, t), dtype)
    dense_spec = pl.BlockSpec((PEER_HEADS, PEER_NKEYS, tile), lambda i: (0, 0, i))
    full = lambda arr: pl.BlockSpec(arr.shape, lambda i: (0,) * arr.ndim)
    return pl.pallas_call(
        functools.partial(_select_kernel, tile=tile),
        grid=(t // tile,),
        in_specs=[pl.BlockSpec((tile, D_MODEL), lambda i: (i, 0)), full(pq), full(keys), full(meta)],
        out_specs=[pl.BlockSpec((D_MODEL, tile), lambda i: (0, i))] + [dense_spec] * 4,
        out_shape=[jax.ShapeDtypeStruct((D_MODEL, t), BF16), dense(F32), dense(F32), dense(BF16), dense(BF16)],
        scratch_shapes=[
            pltpu.VMEM((2 * PEER_HEADS, tile, PEER_KEY_DIM), BF16),
            pltpu.VMEM((PEER_TOPK, LANES), F32),
            pltpu.VMEM((PEER_TOPK, LANES), F32),
        ],
        compiler_params=_params(("parallel",)),
        name="peer_select",
    )(x, pq, keys, meta)


def _activations(xt_ref, u_ref, act_ref):
    z = _dot(u_ref[...], xt_ref[...])
    act_ref[...] = (z + z * lax.erf(z)).astype(BF16)


def _gate_activations(block, e1_ref, c1_ref, e2_ref, r2_ref, act_ref, w_ref, rows_per_block):
    tile = act_ref.shape[1]
    for jj in range(rows_per_block):
        j = block * rows_per_block + jj
        c1 = [jnp.broadcast_to(c1_ref[h, pl.ds(j, 1), :], (BF16_ROWS, tile)).astype(BF16)
              for h in range(PEER_HEADS)]
        e1 = [jnp.broadcast_to(e1_ref[h, pl.ds(j, 1), :], (BF16_ROWS, tile)).astype(BF16)
              for h in range(PEER_HEADS)]
        for g in range(PEER_NKEYS // BF16_ROWS):
            keys = slice(g * BF16_ROWS, (g + 1) * BF16_ROWS)
            gate = None
            for h in range(PEER_HEADS):
                picked = jnp.where(r2_ref[h, keys, :] < c1[h], e2_ref[h, keys, :], jnp.zeros((), BF16))
                term = picked * e1[h]
                gate = term if gate is None else gate + term
            rows = slice(jj * PEER_NKEYS + g * BF16_ROWS, jj * PEER_NKEYS + (g + 1) * BF16_ROWS)
            w_ref[rows, :] = gate * act_ref[rows, :]


def _dense_kernel(xt_ref, e1_ref, c1_ref, e2_ref, r2_ref, u_ref, vtp_ref, *refs, rows_per_block, nb):
    vt_refs, o_ref = refs[:nb], refs[nb]
    act_scr, w_scr = refs[nb + 1:2 * nb + 1], refs[2 * nb + 1:]
    e = pl.program_id(1)
    nrows = rows_per_block * PEER_NKEYS

    @pl.when(e == 0)
    def _():
        o_ref[...] = jnp.zeros_like(o_ref)
        w_scr[nb - 1][...] = jnp.zeros_like(w_scr[nb - 1])

    def activations(i):
        _activations(xt_ref, u_ref.at[i * nrows:(i + 1) * nrows, :], act_scr[i])

    activations(0)
    o_ref[...] += _dot(vtp_ref[...], w_scr[nb - 1][...])
    for i in range(nb):
        _gate_activations(nb * e + i, e1_ref, c1_ref, e2_ref, r2_ref, act_scr[i], w_scr[i], rows_per_block)
        if i + 1 < nb:
            activations(i + 1)
            o_ref[...] += _dot(vt_refs[i][...], w_scr[i][...])

    @pl.when(e == pl.num_programs(1) - 1)
    def _():
        o_ref[...] += _dot(vt_refs[nb - 1][...], w_scr[nb - 1][...])


def _peer_dense(xt, e1, c1, e2, r2, u, vt, tile, eblock, nb):
    t = xt.shape[1]
    nsteps = PEER_N_EXPERTS // (nb * eblock)
    row_spec = pl.BlockSpec((PEER_HEADS, PEER_NKEYS, tile), lambda i, e: (0, 0, i))
    vt_spec = lambda fn: pl.BlockSpec((D_MODEL, eblock), fn)

    def vt_block(i, e, k):
        block = nb * e + k
        return (0, jnp.where(e == nsteps - 1, block, 0) if k == nb - 1 else block)

    return pl.pallas_call(
        functools.partial(_dense_kernel, rows_per_block=eblock // PEER_NKEYS, nb=nb),
        grid=(t // tile, nsteps),
        in_specs=[pl.BlockSpec((D_MODEL, tile), lambda i, e: (0, i))] + [row_spec] * 4 + [
            pl.BlockSpec((nb * eblock, D_MODEL), lambda i, e: (e, 0)),
            vt_spec(lambda i, e: (0, jnp.maximum(nb * e - 1, 0))),
        ] + [vt_spec(functools.partial(vt_block, k=k)) for k in range(nb)],
        out_specs=pl.BlockSpec((D_MODEL, tile), lambda i, e: (0, i)),
        out_shape=jax.ShapeDtypeStruct((D_MODEL, t), F32),
        scratch_shapes=[pltpu.VMEM((eblock, tile), BF16)] * (2 * nb),
        compiler_params=_params(("parallel", "arbitrary")),
        name="peer_dense",
    )(xt, e1, c1, e2, r2, u, vt, *([vt] * nb))


def _final_kernel(x_ref, cmt_ref, p_ref, wg_ref, wp_ref, g_ref, b_ref, o_ref):
    half = x_ref.shape[0] // 2
    for rows in (slice(0, half), slice(half, 2 * half)):
        x = x_ref[rows, :]
        gate = jax.nn.sigmoid(_dot(x.astype(BF16), wg_ref[...]))
        ple = gate * _dot(p_ref[rows, :].astype(BF16), wp_ref[...])
        o_ref[rows, :] = _layer_norm(DEEPNORM_ALPHA * x + cmt_ref[:, rows].T + ple, g_ref[...], b_ref[...])


def _final(x, cmt, p_all, layer, wg, wp, g, b, tile):
    t = x.shape[0]
    full = lambda arr: pl.BlockSpec(arr.shape, lambda i: (0,) * arr.ndim)
    return pl.pallas_call(
        _final_kernel,
        grid=(t // tile,),
        in_specs=[pl.BlockSpec((tile, D_MODEL), lambda i: (i, 0)),
                  pl.BlockSpec((D_MODEL, tile), lambda i: (0, i)),
                  pl.BlockSpec((None, tile, PLE_DIM), lambda i: (layer, i, 0)),
                  full(wg), full(wp), full(g), full(b)],
        out_specs=pl.BlockSpec((tile, D_MODEL), lambda i: (i, 0)),
        out_shape=jax.ShapeDtypeStruct((t, D_MODEL), F32),
        compiler_params=_params(("parallel",)),
        name="ple_ln2",
    )(x, cmt, p_all, wg, wp, g, b)


def _channel_dft():
    c = np.arange(FOURIER_GROUP_DIM)
    ang = 2.0 * np.pi * ((c[:, None] * c[None, :]) % FOURIER_GROUP_DIM) / FOURIER_GROUP_DIM
    eye = np.eye(N_FOURIER_GROUPS)
    return np.concatenate([np.kron(eye, np.cos(ang)), np.kron(eye, -np.sin(ang))], axis=1)


def _sequence_dft(s, scale):
    s_lo = 64
    s_hi = s // s_lo
    k = np.arange(s)
    ang_hi = 2.0 * np.pi * ((k[:, None] * np.arange(s_hi)[None, :]) % s_hi) / s_hi
    ang_lo = 2.0 * np.pi * ((k[:, None] * np.arange(s_lo)[None, :]) % s) / s
    ch = jnp.asarray(np.cos(ang_hi) * scale, F32)[:, :, None]
    sh = jnp.asarray(np.sin(ang_hi) * scale, F32)[:, :, None]
    cl = jnp.asarray(np.cos(ang_lo), F32)[:, None, :]
    sl = jnp.asarray(np.sin(ang_lo), F32)[:, None, :]
    cmat = (ch * cl - sh * sl).reshape(s, s).astype(BF16)
    smat = (sh * cl + ch * sl).reshape(s, s).astype(BF16)
    return cmat, smat


def _attn_bias():
    qi = np.arange(ATTN_BLOCK)
    kj = np.arange(3 * ATTN_BLOCK)
    rel = np.abs(qi[:, None] + ATTN_BLOCK - kj[None, :]).astype(np.float64)
    slopes = np.exp2(-8.0 * np.arange(1, N_HEADS + 1) / N_HEADS)
    bias = np.where(rel[None] <= WINDOW, -slopes[:, None, None] * rel[None], NEG_INF)
    return bias.astype(np.float32)


def _pick_tile(n, pref):
    tile = min(n, pref)
    assert n % tile == 0, (n, tile)
    return tile


def _tiles(t, s):
    return dict(
        rows=_pick_tile(t, 512),
        dft_rows=_pick_tile(s, 512),
        select_tokens=_pick_tile(t, 4 * LANES),
        dense_tokens=_pick_tile(t, 512),
        dense_experts=512,
        dense_blocks=2,
        attn_qblocks=_pick_tile(s // ATTN_BLOCK, 8),
    )


def _trunk(x, p_all, consts, emb_ln, layers):
    bsz, s, _ = x.shape
    t = bsz * s
    mch, cmat, smat, bias = consts
    tiles = _tiles(t, s)
    x = x.reshape(t, D_MODEL)
    p_all = p_all.reshape(p_all.shape[0], t, PLE_DIM)
    for i, lw in enumerate(layers):
        if i == 0:
            x, ab, q, k, v, gt = _inproj(x, emb_ln, lw["w_in"], mch, tiles["rows"])
        else:
            ab, q, k, v, gt = _inproj(x, None, lw["w_in"], mch, tiles["rows"])
        y = _seqdft(ab.reshape(bsz, s, 2 * FOURIER_WIDTH), cmat, smat, tiles["dft_rows"])
        a = _attention(q.reshape(bsz, s, ATTN_WIDTH), k.reshape(bsz, s, KV_WIDTH),
                       v.reshape(bsz, s, KV_WIDTH), bias, lw["sink"], tiles["attn_qblocks"])
        x = _merge(x, y.reshape(t, FOURIER_WIDTH), a.reshape(t, ATTN_WIDTH), gt,
                   lw["w_fo"], lw["w_ao"], lw["w_out"], lw["ln1_g"], lw["ln1_b"], tiles["rows"])
        xt, e1, c1, e2, r2 = _peer_select(x, lw["pq"], lw["keys"], tiles["select_tokens"])
        cmt = _peer_dense(xt, e1, c1, e2, r2, lw["u"], lw["vt"], tiles["dense_tokens"],
                          tiles["dense_experts"], tiles["dense_blocks"])
        x = _final(x, cmt, p_all, i, lw["wg"], lw["wp"], lw["ln2_g"], lw["ln2_b"], tiles["rows"])
    return x.reshape(bsz, s, D_MODEL)


def _prepare_layer(i, w_in, attn_sink, w_fourier_out, w_attn_out, w_out, ln1_g, ln1_b, peer_w_q,
                   peer_keys, peer_u, peer_v, ple_w_gate, ple_w_proj, ln2_g, ln2_b):
    w = w_in[i]
    wq = w[:, _O_Q:_O_K].reshape(D_MODEL, N_KV_HEADS, Q_GROUP, HEAD_DIM).transpose(0, 2, 1, 3)
    wq = wq.reshape(D_MODEL, ATTN_WIDTH) * (HEAD_DIM ** -0.5)
    w_perm = jnp.concatenate([w[:, :_O_Q], wq, w[:, _O_K:]], axis=1).astype(BF16)
    wao = w_attn_out[i].reshape(N_KV_HEADS, Q_GROUP, HEAD_DIM, D_MODEL).transpose(1, 0, 2, 3)
    row = lambda a: a[i].reshape(1, D_MODEL).astype(F32)
    return dict(
        w_in=w_perm, sink=attn_sink[i].astype(F32),
        w_fo=w_fourier_out[i].astype(BF16), w_ao=wao.reshape(ATTN_WIDTH, D_MODEL).astype(BF16),
        w_out=w_out[i].astype(BF16), ln1_g=row(ln1_g), ln1_b=row(ln1_b),
        pq=peer_w_q[i].astype(BF16), keys=peer_keys[i].astype(BF16),
        u=(peer_u[i] * (1.0 / math.sqrt(2.0))).astype(BF16), vt=peer_v[i].astype(BF16).T,
        wg=ple_w_gate[i].astype(BF16), wp=ple_w_proj[i].astype(BF16), ln2_g=row(ln2_g), ln2_b=row(ln2_b),
    )


def kernel(x_prompt, x_sample, p_prompt, p_sample, emb_ln_g, emb_ln_b, w_in, attn_sink, w_fourier_out, w_attn_out, w_out, ln1_g, ln1_b, peer_w_q, peer_keys, peer_u, peer_v, ple_w_gate, ple_w_proj, ln2_g, ln2_b):
    depth = w_in.shape[0]
    layers = [_prepare_layer(i, w_in, attn_sink, w_fourier_out, w_attn_out, w_out, ln1_g, ln1_b, peer_w_q,
                             peer_keys, peer_u, peer_v, ple_w_gate, ple_w_proj, ln2_g, ln2_b)
              for i in range(depth)]
    emb_ln = (emb_ln_g.reshape(1, D_MODEL).astype(F32), emb_ln_b.reshape(1, D_MODEL).astype(F32))
    mch = jnp.asarray(_channel_dft() / math.sqrt(FOURIER_GROUP_DIM), BF16)
    bias = jnp.asarray(_attn_bias())
    outs = []
    for x, p in ((x_prompt, p_prompt), (x_sample, p_sample)):
        s = x.shape[1]
        cmat, smat = _sequence_dft(s, 1.0 / math.sqrt(s))
        outs.append(_trunk(x, p, (mch, cmat, smat, bias), emb_ln, layers))
    return tuple(outs)
```

```python
import functools
import math

import numpy as np
import jax
import jax.numpy as jnp
from jax import lax
from jax.experimental import pallas as pl
from jax.experimental.pallas import tpu as pltpu

F32 = jnp.float32
BF16 = jnp.bfloat16

D_MODEL = 1024
N_HEADS = 8
N_KV_HEADS = 2
Q_GROUP = N_HEADS // N_KV_HEADS
HEAD_DIM = 64
WINDOW = 128
ATTN_BLOCK = 128
ATTN_WIDTH = N_HEADS * HEAD_DIM
KV_WIDTH = N_KV_HEADS * HEAD_DIM
N_FOURIER_GROUPS = 8
FOURIER_GROUP_DIM = 64
FOURIER_WIDTH = N_FOURIER_GROUPS * FOURIER_GROUP_DIM
GATE_WIDTH = 2 * D_MODEL
IN_WIDTH = FOURIER_WIDTH + ATTN_WIDTH + 2 * KV_WIDTH + GATE_WIDTH
PEER_HEADS = 8
PEER_NKEYS = 128
PEER_N_EXPERTS = PEER_NKEYS * PEER_NKEYS
PEER_TOPK = 16
PEER_KEY_DIM = 128
PLE_DIM = 256
DEPTH = 2
DEEPNORM_ALPHA = (2 * DEPTH) ** 0.25
LN_EPS = 1e-5
GELU_FOLD = 1.0 / math.sqrt(2.0)
NEG_INF = -1e30

LANES = 128
SUBLANES = 8
BF16_ROWS = 16
VMEM_LIMIT = 52 * 1024 * 1024

_O_F = 0
_O_Q = _O_F + FOURIER_WIDTH
_O_K = _O_Q + ATTN_WIDTH
_O_V = _O_K + KV_WIDTH
_O_G = _O_V + KV_WIDTH

_CAND_GROUPS = [(0, 0), (0, 8), (1, 0), (2, 0), (3, 0), (4, 0), (5, 0), (6, 0), (7, 0)]
_CAND_ROWS = 8 * (len(_CAND_GROUPS) + 1)
_BIG_INDEX = 1.0e6


def _params(semantics, flags=None):
    return pltpu.CompilerParams(dimension_semantics=semantics, vmem_limit_bytes=VMEM_LIMIT, flags=flags)


def _layer_norm(x, g, b):
    mu = jnp.mean(x, axis=-1, keepdims=True)
    xc = x - mu
    var = jnp.mean(xc * xc, axis=-1, keepdims=True)
    return xc * lax.rsqrt(var + LN_EPS) * g + b


def _dot(a, b):
    return jnp.dot(a, b, preferred_element_type=F32)


def _dot_nt(a, b):
    return lax.dot_general(a, b, (((1,), (1,)), ((), ())), preferred_element_type=F32)


def _inproj_kernel(*refs, apply_ln):
    if apply_ln:
        x_ref, g_ref, b_ref, w_ref, mch_ref, xn_ref, ab_ref, q_ref, k_ref, v_ref, gt_ref = refs
        x = _layer_norm(x_ref[...], g_ref[...], b_ref[...])
        xn_ref[...] = x
    else:
        x_ref, w_ref, mch_ref, ab_ref, q_ref, k_ref, v_ref, gt_ref = refs
        x = x_ref[...]
    xb = x.astype(BF16)
    f = _dot(xb, w_ref[:, _O_F:_O_Q])
    ab_ref[...] = _dot(f.astype(BF16), mch_ref[...]).astype(BF16)
    q_ref[...] = _dot(xb, w_ref[:, _O_Q:_O_K]).astype(BF16)
    k_ref[...] = _dot(xb, w_ref[:, _O_K:_O_V]).astype(BF16)
    v_ref[...] = _dot(xb, w_ref[:, _O_V:_O_G]).astype(BF16)
    gt_ref[...] = jax.nn.sigmoid(_dot(xb, w_ref[:, _O_G:IN_WIDTH])).astype(BF16)


def _inproj(x, ln, w_in, mch, tile):
    t = x.shape[0]
    apply_ln = ln is not None
    row = lambda width: pl.BlockSpec((tile, width), lambda i: (i, 0))
    full = lambda a: pl.BlockSpec(a.shape, lambda i: (0,) * a.ndim)
    ins = [x] + ([ln[0], ln[1]] if apply_ln else []) + [w_in, mch]
    in_specs = [row(D_MODEL)] + ([full(ln[0]), full(ln[1])] if apply_ln else []) + [full(w_in), full(mch)]
    widths = [2 * FOURIER_WIDTH, ATTN_WIDTH, KV_WIDTH, KV_WIDTH, GATE_WIDTH]
    out_shape = [jax.ShapeDtypeStruct((t, w), BF16) for w in widths]
    out_specs = [row(w) for w in widths]
    if apply_ln:
        out_shape = [jax.ShapeDtypeStruct((t, D_MODEL), F32)] + out_shape
        out_specs = [row(D_MODEL)] + out_specs
    return pl.pallas_call(
        functools.partial(_inproj_kernel, apply_ln=apply_ln),
        grid=(t // tile,),
        in_specs=in_specs,
        out_specs=out_specs,
        out_shape=out_shape,
        compiler_params=_params(("parallel",)),
        name="inproj_ln" if apply_ln else "inproj",
    )(*ins)


def _seqdft_kernel(c_ref, s_ref, a_ref, b_ref, y_ref):
    y_ref[...] = (_dot(c_ref[...], a_ref[...]) + _dot(s_ref[...], b_ref[...])).astype(BF16)


def _seqdft(ab, cmat, smat, tile):
    bsz, s, _ = ab.shape
    return pl.pallas_call(
        _seqdft_kernel,
        grid=(bsz, s // tile),
        in_specs=[
            pl.BlockSpec((tile, s), lambda b, m: (m, 0)),
            pl.BlockSpec((tile, s), lambda b, m: (m, 0)),
            pl.BlockSpec((None, s, FOURIER_WIDTH), lambda b, m: (b, 0, 0)),
            pl.BlockSpec((None, s, FOURIER_WIDTH), lambda b, m: (b, 0, 1)),
        ],
        out_specs=pl.BlockSpec((None, tile, FOURIER_WIDTH), lambda b, m: (b, m, 0)),
        out_shape=jax.ShapeDtypeStruct((bsz, s, FOURIER_WIDTH), BF16),
        compiler_params=_params(("parallel", "parallel")),
        name="seqdft",
    )(cmat, smat, ab, ab)


def _attn_kernel(sink_ref, q_ref, kp_ref, kc_ref, kn_ref, vp_ref, vc_ref, vn_ref, bias_ref, o_ref, *, nsteps, qblocks):
    n = pl.program_id(1)
    col = lax.broadcasted_iota(jnp.int32, (1, 3 * ATTN_BLOCK), 1)
    kall = jnp.concatenate([kp_ref[...], kc_ref[...], kn_ref[...]], axis=0)
    vall = jnp.concatenate([vp_ref[...], vc_ref[...], vn_ref[...]], axis=0)
    vones = jnp.concatenate([vall, jnp.ones_like(vall)], axis=1)
    lane = lax.broadcasted_iota(jnp.int32, (ATTN_BLOCK, KV_WIDTH), 1)
    low = lane < HEAD_DIM
    for j in range(qblocks):
        rows = slice(j * ATTN_BLOCK, (j + 1) * ATTN_BLOCK)
        window = slice(j * ATTN_BLOCK, (j + 3) * ATTN_BLOCK)
        off_edge = None
        if j == 0:
            off_edge = (col < ATTN_BLOCK) & (n == 0)
        if j == qblocks - 1:
            after = (col >= 2 * ATTN_BLOCK) & (n == nsteps - 1)
            off_edge = after if off_edge is None else off_edge | after
        kcat = kall[window]
        vcat = vones[window]
        for g in range(Q_GROUP):
            qg = q_ref[rows, g * KV_WIDTH:(g + 1) * KV_WIDTH]
            halves = []
            for kh in range(N_KV_HEADS):
                h = kh * Q_GROUP + g
                qm = jnp.where(low if kh == 0 else jnp.logical_not(low), qg, jnp.zeros_like(qg))
                s = _dot_nt(qm, kcat) + bias_ref[h]
                if off_edge is not None:
                    s = s + jnp.where(off_edge, NEG_INF, 0.0).astype(F32)
                sink = sink_ref[h]
                m = jnp.maximum(jnp.max(s, axis=-1, keepdims=True), sink)
                p = jnp.exp(s - m)
                pv = _dot(p.astype(BF16), vcat)
                halves.append(pv[:, :KV_WIDTH] / (pv[:, KV_WIDTH:] + jnp.exp(sink - m)))
            o_ref[rows, g * KV_WIDTH:(g + 1) * KV_WIDTH] = jnp.where(low, halves[0], halves[1]).astype(BF16)


def _attention(q, k, v, bias, sink, qblocks):
    bsz, s, _ = q.shape
    nb = s // ATTN_BLOCK
    nsteps = nb // qblocks
    edge_spec = lambda fn: pl.BlockSpec((None, ATTN_BLOCK, KV_WIDTH), fn)
    prev = lambda b, n: (b, jnp.maximum(qblocks * n - 1, 0), 0)
    cur = lambda b, n: (b, n, 0)
    nxt = lambda b, n: (b, jnp.minimum(qblocks * (n + 1), nb - 1), 0)
    mid_spec = pl.BlockSpec((None, qblocks * ATTN_BLOCK, KV_WIDTH), cur)
    return pl.pallas_call(
        functools.partial(_attn_kernel, nsteps=nsteps, qblocks=qblocks),
        grid=(bsz, nsteps),
        in_specs=[
            pl.BlockSpec(memory_space=pltpu.SMEM),
            pl.BlockSpec((None, qblocks * ATTN_BLOCK, ATTN_WIDTH), cur),
            edge_spec(prev), mid_spec, edge_spec(nxt),
            edge_spec(prev), mid_spec, edge_spec(nxt),
            pl.BlockSpec(bias.shape, lambda b, n: (0, 0, 0)),
        ],
        out_specs=pl.BlockSpec((None, qblocks * ATTN_BLOCK, ATTN_WIDTH), cur),
        out_shape=jax.ShapeDtypeStruct((bsz, s, ATTN_WIDTH), BF16),
        compiler_params=_params(("parallel", "parallel")),
        name="window_attn",
    )(sink, q, k, k, k, v, v, v, bias)


def _merge_kernel(x_ref, y_ref, a_ref, gt_ref, wfo_ref, wao_ref, wout_ref, g_ref, b_ref, o_ref):
    f = _dot(y_ref[...], wfo_ref[...])
    a = _dot(a_ref[...], wao_ref[...])
    merged = gt_ref[:, :D_MODEL].astype(F32) * f + gt_ref[:, D_MODEL:].astype(F32) * a
    o = _dot(merged.astype(BF16), wout_ref[...])
    o_ref[...] = _layer_norm(DEEPNORM_ALPHA * x_ref[...] + o, g_ref[...], b_ref[...])


def _merge(x, y, a, gt, wfo, wao, wout, g, b, tile):
    t = x.shape[0]
    row = lambda width: pl.BlockSpec((tile, width), lambda i: (i, 0))
    full = lambda arr: pl.BlockSpec(arr.shape, lambda i: (0,) * arr.ndim)
    return pl.pallas_call(
        _merge_kernel,
        grid=(t // tile,),
        in_specs=[row(D_MODEL), row(FOURIER_WIDTH), row(ATTN_WIDTH), row(GATE_WIDTH),
                  full(wfo), full(wao), full(wout), full(g), full(b)],
        out_specs=row(D_MODEL),
        out_shape=jax.ShapeDtypeStruct((t, D_MODEL), F32),
        compiler_params=_params(("parallel",)),
        name="merge_ln1",
    )(x, y, a, gt, wfo, wao, wout, g, b)


def _extract_sorted(s, ids, count, vals_ref):
    def body(r, carry):
        s, rank = carry
        m = jnp.max(s, axis=0, keepdims=True)
        first = jnp.min(jnp.where(s == m, ids, _BIG_INDEX), axis=0, keepdims=True)
        sel = ids == first
        if vals_ref is not None:
            vals_ref[pl.ds(r, 1), :] = m
        rank = jnp.where(sel, lax.convert_element_type(r, F32), rank)
        s = jnp.where(sel, -jnp.inf, s)
        return s, rank

    rank0 = jnp.full(s.shape, float(count), F32)
    s, rank = lax.fori_loop(0, count, body, (s, rank0))
    return rank, s


def _select_exact(s1, s2, key_ids, cand_ids, cand_mask, v1_scr, v2_scr):
    rank1, _ = _extract_sorted(s1, key_ids, PEER_TOPK, v1_scr)
    rank2, _ = _extract_sorted(s2, key_ids, PEER_TOPK, v2_scr)
    v1 = v1_scr[...]
    v2 = v2_scr[...]
    groups = [v1[r1:r1 + 1] + v2[lo:lo + SUBLANES] for r1, lo in _CAND_GROUPS]
    groups.append(v1[SUBLANES:] + v2[0:1])
    cand = jnp.concatenate(groups, axis=0) + cand_mask
    crank, _ = _extract_sorted(cand, cand_ids, PEER_TOPK, None)
    chosen = crank < float(PEER_TOPK)
    pexp = jnp.where(chosen, jnp.exp(cand - cand[0:1]), 0.0)
    inv_z = GELU_FOLD / jnp.sum(pexp, axis=0, keepdims=True)
    cnt = chosen.astype(F32)
    counts = [jnp.sum(cnt[0:2 * SUBLANES], axis=0, keepdims=True)]
    for gi in range(2, len(_CAND_GROUPS)):
        counts.append(jnp.sum(cnt[gi * SUBLANES:(gi + 1) * SUBLANES], axis=0, keepdims=True))
    tail = cnt[len(_CAND_GROUPS) * SUBLANES:]
    c1 = jnp.zeros_like(s1)
    for r in range(PEER_TOPK):
        cr = counts[r] if r < SUBLANES else tail[r - SUBLANES:r - SUBLANES + 1]
        c1 = jnp.where(rank1 == float(r), cr, c1)
    e1 = jnp.exp(s1 - v1[0:1]) * inv_z
    e2 = jnp.exp(s2 - v2[0:1])
    return e1, c1, e2, rank2


def _merge_exchange_pairs(n):
    pairs = []
    t = max(1, math.ceil(math.log2(n)))
    p = 1 << (t - 1)
    while p > 0:
        q, r, d = 1 << (t - 1), 0, p
        while d > 0:
            pairs.extend((i, i + d) for i in range(n - d) if (i & p) == r)
            d, q, r = q - p, q >> 1, p
        p >>= 1
    return pairs


def _compare_exchange(v, i, j):
    v[i], v[j] = jnp.maximum(v[i], v[j]), jnp.minimum(v[i], v[j])


def _sort_desc(v):
    v = list(v)
    for i, j in _merge_exchange_pairs(len(v)):
        _compare_exchange(v, i, j)
    return v


def _bitonic_merge(v):
    v = list(v)
    d = len(v) // 2
    while d:
        for k in range(len(v)):
            if not k & d:
                _compare_exchange(v, k, k + d)
        d //= 2
    return v


def _across_sublanes(x, op):
    for shift in (4, 2, 1):
        x = op(x, pltpu.roll(x, shift, 0))
    return x


def _top16_of_sublane_lists(lists):
    for shift in (4, 2, 1):
        other = [pltpu.roll(x, shift, 0) for x in lists]
        n = len(lists)
        merged = []
        for k in range(PEER_TOPK):
            mine = lists[k] if k < n else None
            theirs = other[PEER_TOPK - 1 - k] if PEER_TOPK - 1 - k < n else None
            merged.append(mine if theirs is None else theirs if mine is None else jnp.maximum(mine, theirs))
        lists = _bitonic_merge(merged)
    return lists


def _rank_among_sorted(a, v):
    assert len(v) == PEER_TOPK == 16
    m1 = a >= v[7]
    m2 = a >= jnp.where(m1, v[3], v[11])
    m3 = a >= jnp.where(m1, jnp.where(m2, v[1], v[5]), jnp.where(m2, v[9], v[13]))
    t4 = jnp.where(m1,
                   jnp.where(m2, jnp.where(m3, v[0], v[2]), jnp.where(m3, v[4], v[6])),
                   jnp.where(m2, jnp.where(m3, v[8], v[10]), jnp.where(m3, v[12], v[14])))
    m4 = a >= t4
    m5 = a >= v[15]
    rank = (jnp.where(m1, 0.0, 8.0) + jnp.where(m2, 0.0, 4.0)) + (jnp.where(m3, 0.0, 2.0) + jnp.where(m4, 0.0, 1.0))
    return rank + jnp.where(m5, 0.0, 1.0)


def _select_sorted(s1, s2, cand_mask):
    nv = PEER_NKEYS // SUBLANES
    a1 = [s1[j * SUBLANES:(j + 1) * SUBLANES] for j in range(nv)]
    a2 = [s2[j * SUBLANES:(j + 1) * SUBLANES] for j in range(nv)]
    v1 = _top16_of_sublane_lists(_sort_desc(a1))
    v2 = _top16_of_sublane_lists(_sort_desc(a2))
    sub = lax.broadcasted_iota(jnp.int32, (SUBLANES, LANES), 0)

    def by_sublane(vals):
        out = vals[SUBLANES - 1]
        for j in range(SUBLANES - 2, -1, -1):
            out = jnp.where(sub == j, vals[j], out)
        return out

    v2_nat = {0: by_sublane(v2[:SUBLANES]), SUBLANES: by_sublane(v2[SUBLANES:])}
    groups = [v1[r1] + v2_nat[lo] for r1, lo in _CAND_GROUPS]
    groups.append(by_sublane(v1[SUBLANES:]) + v2[0])
    groups = [g + cand_mask[i * SUBLANES:(i + 1) * SUBLANES] for i, g in enumerate(groups)]
    top = _top16_of_sublane_lists(_sort_desc(groups))
    thr = top[PEER_TOPK - 1]
    z = None
    for k in range(PEER_TOPK):
        term = jnp.exp(top[k] - top[0])
        z = term if z is None else z + term
    inv_z = GELU_FOLD / z
    picked = [jnp.where(g >= thr, 1.0, 0.0) for g in groups[:-1]]
    add = lambda a, b: a + b
    counts = [_across_sublanes(picked[0] + picked[1], add)]
    counts += [_across_sublanes(picked[r + 1], add) for r in range(1, SUBLANES)]
    counts += [jnp.where(v1[r] + v2[0] >= thr, 1.0, 0.0) for r in range(SUBLANES, PEER_TOPK)]
    e1, c1, e2, r2 = [], [], [], []
    c_total = None
    r_total = None
    for j in range(nv):
        c = jnp.zeros((SUBLANES, LANES), F32)
        for k in range(PEER_TOPK - 1, -1, -1):
            c = jnp.where(a1[j] >= v1[k], counts[k], c)
        r = _rank_among_sorted(a2[j], v2)
        c1.append(c)
        r2.append(r)
        e1.append(jnp.exp(a1[j] - v1[0]) * inv_z)
        e2.append(jnp.exp(a2[j] - v2[0]))
        c_total = c if c_total is None else c_total + c
        r_total = (float(PEER_TOPK) - r) if r_total is None else r_total + (float(PEER_TOPK) - r)
    c_total = _across_sublanes(c_total, add)
    r_total = _across_sublanes(r_total, add)
    distinct_total = float(PEER_TOPK * (PEER_TOPK + 1) // 2)
    bad = jnp.where((c_total != float(PEER_TOPK)) | (r_total != distinct_total), 1.0, 0.0)
    cat = lambda parts: jnp.concatenate(parts, axis=0)
    return cat(e1), cat(c1), cat(e2), cat(r2), bad


def _select_kernel(x_ref, pq_ref, keys_ref, meta_ref, xt_ref, e1_ref, c1_ref, e2_ref, r2_ref,
                   q_scr, v1_scr, v2_scr, *, tile):
    x = x_ref[...]
    xt_ref[...] = x.astype(BF16)
    q = _dot(x.astype(BF16), pq_ref[...])
    for hp in range(2 * PEER_HEADS):
        q_scr[hp] = q[:, hp * PEER_KEY_DIM:(hp + 1) * PEER_KEY_DIM].astype(BF16)

    chunks = [slice(c * LANES, (c + 1) * LANES) for c in range(tile // LANES)]

    def scores(h):
        return tuple(_dot_nt(keys_ref[h, half], q_scr[2 * h + half, tok, :]) for tok in chunks for half in (0, 1))

    def head_body(h, s_all):
        s_next = scores(jnp.minimum(h + 1, PEER_HEADS - 1))

        def emit(tok, e1, c1, e2, r2):
            e1_ref[h, :, tok] = e1
            c1_ref[h, :, tok] = c1
            e2_ref[h, :, tok] = e2.astype(BF16)
            r2_ref[h, :, tok] = r2.astype(BF16)

        flags = []
        for c, tok in enumerate(chunks):
            e1, c1, e2, r2, bad = _select_sorted(s_all[2 * c], s_all[2 * c + 1], meta_ref[1])
            emit(tok, e1, c1, e2, r2)
            flags.append(jnp.max(bad))
        for c, tok in enumerate(chunks):
            @pl.when(flags[c] > 0.0)
            def _():
                key_ids = lax.broadcasted_iota(jnp.int32, (PEER_NKEYS, LANES), 0).astype(F32)
                emit(tok, *_select_exact(s_all[2 * c], s_all[2 * c + 1], key_ids, meta_ref[0], meta_ref[1],
                                         v1_scr, v2_scr))
        return s_next

    lax.fori_loop(0, PEER_HEADS, head_body, scores(0))


def _cand_meta():
    ids = np.full((_CAND_ROWS, LANES), _BIG_INDEX, np.float32)
    mask = np.full((_CAND_ROWS, LANES), -np.inf, np.float32)
    for gi, (r1, lo) in enumerate(_CAND_GROUPS):
        for j in range(SUBLANES):
            r2 = lo + j
            if (r1 + 1) * (r2 + 1) <= PEER_TOPK:
                ids[gi * SUBLANES + j] = r1 * PEER_TOPK + r2
                mask[gi * SUBLANES + j] = 0.0
    base = len(_CAND_GROUPS) * SUBLANES
    for j in range(SUBLANES):
        ids[base + j] = (SUBLANES + j) * PEER_TOPK
        mask[base + j] = 0.0
    return np.stack([ids, mask])


def _peer_select(x, pq, keys, tile):
    t = x.shape[0]
    meta = jnp.asarray(_cand_meta())
    dense = lambda dtype: jax.ShapeDtypeStruct((PEER_HEADS, PEER_NKEYS, t), dtype)
    dense_spec = pl.BlockSpec((PEER_HEADS, PEER_NKEYS, tile), lambda i: (0, 0, i))
    full = lambda arr: pl.BlockSpec(arr.shape, lambda i: (0,) * arr.ndim)
    return pl.pallas_call(
        functools.partial(_select_kernel, tile=tile),
        grid=(t // tile,),
        in_specs=[pl.BlockSpec((tile, D_MODEL), lambda i: (i, 0)), full(pq), full(keys), full(meta)],
        out_specs=[pl.BlockSpec((tile, D_MODEL), lambda i: (i, 0))] + [dense_spec] * 4,
        out_shape=[jax.ShapeDtypeStruct((t, D_MODEL), BF16), dense(F32), dense(F32), dense(BF16), dense(BF16)],
        scratch_shapes=[
            pltpu.VMEM((2 * PEER_HEADS, tile, PEER_KEY_DIM), BF16),
            pltpu.VMEM((PEER_TOPK, LANES), F32),
            pltpu.VMEM((PEER_TOPK, LANES), F32),
        ],
        compiler_params=_params(("parallel",)),
        name="peer_select",
    )(x, pq, keys, meta)


def _activations(xt_ref, u_ref, act_ref):
    z = _dot_nt(u_ref[...], xt_ref[...])
    act_ref[...] = (z + z * lax.erf(z)).astype(BF16)


def _gate_activations(block, e1_ref, c1_ref, e2_ref, r2_ref, act_ref, w_ref, rows_per_block):
    tile = act_ref.shape[1]
    for jj in range(rows_per_block):
        j = block * rows_per_block + jj
        c1 = [jnp.broadcast_to(c1_ref[h, pl.ds(j, 1), :], (BF16_ROWS, tile)).astype(BF16)
              for h in range(PEER_HEADS)]
        e1 = [jnp.broadcast_to(e1_ref[h, pl.ds(j, 1), :], (BF16_ROWS, tile)).astype(BF16)
              for h in range(PEER_HEADS)]
        for g in range(PEER_NKEYS // BF16_ROWS):
            keys = slice(g * BF16_ROWS, (g + 1) * BF16_ROWS)
            gate = None
            for h in range(PEER_HEADS):
                picked = jnp.where(r2_ref[h, keys, :] < c1[h], e2_ref[h, keys, :], jnp.zeros((), BF16))
                term = picked * e1[h]
                gate = term if gate is None else gate + term
            rows = slice(jj * PEER_NKEYS + g * BF16_ROWS, jj * PEER_NKEYS + (g + 1) * BF16_ROWS)
            w_ref[rows, :] = gate * act_ref[rows, :]


def _dense_kernel(xt_ref, e1_ref, c1_ref, e2_ref, r2_ref, u_ref, vtp_ref, *refs, rows_per_block, nb):
    vt_refs, o_ref = refs[:nb], refs[nb]
    act_scr, w_scr = refs[nb + 1:2 * nb + 1], refs[2 * nb + 1:]
    e = pl.program_id(1)
    nrows = rows_per_block * PEER_NKEYS

    @pl.when(e == 0)
    def _():
        o_ref[...] = jnp.zeros_like(o_ref)
        w_scr[nb - 1][...] = jnp.zeros_like(w_scr[nb - 1])

    def activations(i):
        _activations(xt_ref, u_ref.at[i * nrows:(i + 1) * nrows, :], act_scr[i])

    activations(0)
    o_ref[...] += _dot(vtp_ref[...], w_scr[nb - 1][...])
    for i in range(nb):
        _gate_activations(nb * e + i, e1_ref, c1_ref, e2_ref, r2_ref, act_scr[i], w_scr[i], rows_per_block)
        if i + 1 < nb:
            activations(i + 1)
            o_ref[...] += _dot(vt_refs[i][...], w_scr[i][...])

    @pl.when(e == pl.num_programs(1) - 1)
    def _():
        o_ref[...] += _dot(vt_refs[nb - 1][...], w_scr[nb - 1][...])


def _peer_dense(xt, e1, c1, e2, r2, u, vt, tile, eblock, nb):
    t = xt.shape[0]
    nsteps = PEER_N_EXPERTS // (nb * eblock)
    row_spec = pl.BlockSpec((PEER_HEADS, PEER_NKEYS, tile), lambda i, e: (0, 0, i))
    vt_spec = lambda fn: pl.BlockSpec((D_MODEL, eblock), fn)

    def vt_block(i, e, k):
        block = nb * e + k
        return (0, jnp.where(e == nsteps - 1, block, 0) if k == nb - 1 else block)

    return pl.pallas_call(
        functools.partial(_dense_kernel, rows_per_block=eblock // PEER_NKEYS, nb=nb),
        grid=(t // tile, nsteps),
        in_specs=[pl.BlockSpec((tile, D_MODEL), lambda i, e: (i, 0))] + [row_spec] * 4 + [
            pl.BlockSpec((nb * eblock, D_MODEL), lambda i, e: (e, 0)),
            vt_spec(lambda i, e: (0, jnp.maximum(nb * e - 1, 0))),
        ] + [vt_spec(functools.partial(vt_block, k=k)) for k in range(nb)],
        out_specs=pl.BlockSpec((D_MODEL, tile), lambda i, e: (0, i)),
        out_shape=jax.ShapeDtypeStruct((D_MODEL, t), F32),
        scratch_shapes=[pltpu.VMEM((eblock, tile), BF16)] * (2 * nb),
        compiler_params=_params(("parallel", "arbitrary")),
        name="peer_dense",
    )(xt, e1, c1, e2, r2, u, vt, *([vt] * nb))


def _final_kernel(x_ref, cmt_ref, p_ref, wg_ref, wp_ref, g_ref, b_ref, o_ref):
    x = x_ref[...]
    gate = jax.nn.sigmoid(_dot(x.astype(BF16), wg_ref[...]))
    ple = gate * _dot(p_ref[...].astype(BF16), wp_ref[...])
    o_ref[...] = _layer_norm(DEEPNORM_ALPHA * x + cmt_ref[...].T + ple, g_ref[...], b_ref[...])


def _final(x, cmt, p_all, layer, wg, wp, g, b, tile):
    t = x.shape[0]
    full = lambda arr: pl.BlockSpec(arr.shape, lambda i: (0,) * arr.ndim)
    return pl.pallas_call(
        _final_kernel,
        grid=(t // tile,),
        in_specs=[pl.BlockSpec((tile, D_MODEL), lambda i: (i, 0)),
                  pl.BlockSpec((D_MODEL, tile), lambda i: (0, i)),
                  pl.BlockSpec((None, tile, PLE_DIM), lambda i: (layer, i, 0)),
                  full(wg), full(wp), full(g), full(b)],
        out_specs=pl.BlockSpec((tile, D_MODEL), lambda i: (i, 0)),
        out_shape=jax.ShapeDtypeStruct((t, D_MODEL), F32),
        compiler_params=_params(("parallel",)),
        name="ple_ln2",
    )(x, cmt, p_all, wg, wp, g, b)


def _channel_dft():
    c = np.arange(FOURIER_GROUP_DIM)
    ang = 2.0 * np.pi * ((c[:, None] * c[None, :]) % FOURIER_GROUP_DIM) / FOURIER_GROUP_DIM
    eye = np.eye(N_FOURIER_GROUPS)
    return np.concatenate([np.kron(eye, np.cos(ang)), np.kron(eye, -np.sin(ang))], axis=1)


def _sequence_dft(s, scale):
    s_lo = 64
    s_hi = s // s_lo
    k = np.arange(s)
    ang_hi = 2.0 * np.pi * ((k[:, None] * np.arange(s_hi)[None, :]) % s_hi) / s_hi
    ang_lo = 2.0 * np.pi * ((k[:, None] * np.arange(s_lo)[None, :]) % s) / s
    ch = jnp.asarray(np.cos(ang_hi) * scale, F32)[:, :, None]
    sh = jnp.asarray(np.sin(ang_hi) * scale, F32)[:, :, None]
    cl = jnp.asarray(np.cos(ang_lo), F32)[:, None, :]
    sl = jnp.asarray(np.sin(ang_lo), F32)[:, None, :]
    cmat = (ch * cl - sh * sl).reshape(s, s).astype(BF16)
    smat = (sh * cl + ch * sl).reshape(s, s).astype(BF16)
    return cmat, smat


def _attn_bias():
    qi = np.arange(ATTN_BLOCK)
    kj = np.arange(3 * ATTN_BLOCK)
    rel = np.abs(qi[:, None] + ATTN_BLOCK - kj[None, :]).astype(np.float64)
    slopes = np.exp2(-8.0 * np.arange(1, N_HEADS + 1) / N_HEADS)
    bias = np.where(rel[None] <= WINDOW, -slopes[:, None, None] * rel[None], NEG_INF)
    return bias.astype(np.float32)


def _pick_tile(n, pref):
    tile = min(n, pref)
    assert n % tile == 0, (n, tile)
    return tile


def _tiles(t, s):
    return dict(
        rows=_pick_tile(t, 512),
        dft_rows=_pick_tile(s, 512),
        select_tokens=_pick_tile(t, 4 * LANES),
        dense_tokens=_pick_tile(t, 512),
        dense_experts=512,
        dense_blocks=2,
        attn_qblocks=_pick_tile(s // ATTN_BLOCK, 8),
    )


def _trunk(x, p_all, consts, emb_ln, layers):
    bsz, s, _ = x.shape
    t = bsz * s
    mch, cmat, smat, bias = consts
    tiles = _tiles(t, s)
    x = x.reshape(t, D_MODEL)
    p_all = p_all.reshape(p_all.shape[0], t, PLE_DIM)
    for i, lw in enumerate(layers):
        if i == 0:
            x, ab, q, k, v, gt = _inproj(x, emb_ln, lw["w_in"], mch, tiles["rows"])
        else:
            ab, q, k, v, gt = _inproj(x, None, lw["w_in"], mch, tiles["rows"])
        y = _seqdft(ab.reshape(bsz, s, 2 * FOURIER_WIDTH), cmat, smat, tiles["dft_rows"])
        a = _attention(q.reshape(bsz, s, ATTN_WIDTH), k.reshape(bsz, s, KV_WIDTH),
                       v.reshape(bsz, s, KV_WIDTH), bias, lw["sink"], tiles["attn_qblocks"])
        x = _merge(x, y.reshape(t, FOURIER_WIDTH), a.reshape(t, ATTN_WIDTH), gt,
                   lw["w_fo"], lw["w_ao"], lw["w_out"], lw["ln1_g"], lw["ln1_b"], tiles["rows"])
        xt, e1, c1, e2, r2 = _peer_select(x, lw["pq"], lw["keys"], tiles["select_tokens"])
        cmt = _peer_dense(xt, e1, c1, e2, r2, lw["u"], lw["vt"], tiles["dense_tokens"],
                          tiles["dense_experts"], tiles["dense_blocks"])
        x = _final(x, cmt, p_all, i, lw["wg"], lw["wp"], lw["ln2_g"], lw["ln2_b"], tiles["rows"])
    return x.reshape(bsz, s, D_MODEL)


def _prepare_layer(i, w_in, attn_sink, w_fourier_out, w_attn_out, w_out, ln1_g, ln1_b, peer_w_q,
                   peer_keys, peer_u, peer_v, ple_w_gate, ple_w_proj, ln2_g, ln2_b):
    w = w_in[i]
    wq = w[:, _O_Q:_O_K].reshape(D_MODEL, N_KV_HEADS, Q_GROUP, HEAD_DIM).transpose(0, 2, 1, 3)
    wq = wq.reshape(D_MODEL, ATTN_WIDTH) * (HEAD_DIM ** -0.5)
    w_perm = jnp.concatenate([w[:, :_O_Q], wq, w[:, _O_K:]], axis=1).astype(BF16)
    wao = w_attn_out[i].reshape(N_KV_HEADS, Q_GROUP, HEAD_DIM, D_MODEL).transpose(1, 0, 2, 3)
    row = lambda a: a[i].reshape(1, D_MODEL).astype(F32)
    return dict(
        w_in=w_perm, sink=attn_sink[i].astype(F32),
        w_fo=w_fourier_out[i].astype(BF16), w_ao=wao.reshape(ATTN_WIDTH, D_MODEL).astype(BF16),
        w_out=w_out[i].astype(BF16), ln1_g=row(ln1_g), ln1_b=row(ln1_b),
        pq=peer_w_q[i].astype(BF16), keys=peer_keys[i].astype(BF16),
        u=(peer_u[i] * (1.0 / math.sqrt(2.0))).astype(BF16), vt=peer_v[i].astype(BF16).T,
        wg=ple_w_gate[i].astype(BF16), wp=ple_w_proj[i].astype(BF16), ln2_g=row(ln2_g), ln2_b=row(ln2_b),
    )


def kernel(x_prompt, x_sample, p_prompt, p_sample, emb_ln_g, emb_ln_b, w_in, attn_sink, w_fourier_out, w_attn_out, w_out, ln1_g, ln1_b, peer_w_q, peer_keys, peer_u, peer_v, ple_w_gate, ple_w_proj, ln2_g, ln2_b):
    depth = w_in.shape[0]
    layers = [_prepare_layer(i, w_in, attn_sink, w_fourier_out, w_attn_out, w_out, ln1_g, ln1_b, peer_w_q,
                             peer_keys, peer_u, peer_v, ple_w_gate, ple_w_proj, ln2_g, ln2_b)
              for i in range(depth)]
    emb_ln = (emb_ln_g.reshape(1, D_MODEL).astype(F32), emb_ln_b.reshape(1, D_MODEL).astype(F32))
    mch = jnp.asarray(_channel_dft() / math.sqrt(FOURIER_GROUP_DIM), BF16)
    bias = jnp.asarray(_attn_bias())
    outs = []
    for x, p in ((x_prompt, p_prompt), (x_sample, p_sample)):
        s = x.shape[1]
        cmat, smat = _sequence_dft(s, 1.0 / math.sqrt(s))
        outs.append(_trunk(x, p, (mch, cmat, smat, bias), emb_ln, layers))
    return tuple(outs)
```
